```python
import math
import jax, jax.numpy as jnp
from jax import lax
import numpy as np

D_MODEL = 1024
BATCH = 2
SEQ = 8192
DEPTH = 1

CTX_LEN = 256
GRID_W = 64
HEAD_DIM = 64
GQA_HEADS = 8
GQA_KV_HEADS = 2
GQA_GROUP = GQA_HEADS // GQA_KV_HEADS
DIFF_HEADS = 4
DIFF_V_DIM = 2 * HEAD_DIM
MIX_WIDTH = GQA_HEADS * HEAD_DIM + DIFF_HEADS * DIFF_V_DIM
A_Q_COLS = GQA_HEADS * HEAD_DIM
A_KV_COLS = GQA_KV_HEADS * HEAD_DIM
B_QK_COLS = DIFF_HEADS * 2 * HEAD_DIM
B_V_COLS = DIFF_HEADS * DIFF_V_DIM
QKV_COLS = A_Q_COLS + 2 * A_KV_COLS + 2 * B_QK_COLS + B_V_COLS
Q_BLOCK = 128
ROPE_THETA = 10000.0
N_EXPERTS = 32
TOP_K = 4
D_FF = D_MODEL
SWIGLU_ALPHA = 1.702
SWIGLU_LIMIT = 7.0
EXPERT_BLOCK = 128
N_MOD = 6
EPS = 1e-6

kernel_name = 'hybrid_gqa_diffattn_moe_dit_layer'


def rms_norm(x, g):
    xf = x.astype(jnp.float32)
    y = xf * lax.rsqrt(jnp.mean(xf * xf, axis=-1, keepdims=True) + EPS)
    return (y * g.astype(jnp.float32)).astype(x.dtype)


def modulate(h, shift, scale):
    return h * (1 + scale) + shift


def axial_rope_tables(n_tokens):
    rows = n_tokens // GRID_W
    row = jnp.broadcast_to(jnp.arange(rows, dtype=jnp.float32)[:, None], (rows, GRID_W)).reshape(-1)
    col = jnp.broadcast_to(jnp.arange(GRID_W, dtype=jnp.float32)[None, :], (rows, GRID_W)).reshape(-1)
    n_freq = HEAD_DIM // 4
    inv_freq = ROPE_THETA ** (-jnp.arange(n_freq, dtype=jnp.float32) / n_freq)
    ang = jnp.stack([row[:, None] * inv_freq, col[:, None] * inv_freq], axis=1)
    return jnp.cos(ang), jnp.sin(ang)


def apply_axial_rope(x, cos, sin):
    b, t, h, _ = x.shape
    xr = x.reshape(b, t, h, 2, 2, HEAD_DIM // 4)
    x1, x2 = xr[..., 0, :], xr[..., 1, :]
    cs = cos[None, :, None]
    sn = sin[None, :, None]
    out = jnp.stack([x1 * cs - x2 * sn, x2 * cs + x1 * sn], axis=-2)
    return out.reshape(x.shape).astype(x.dtype)


def split_projection(p):
    b, t, _ = p.shape
    offs = [A_Q_COLS, A_Q_COLS + A_KV_COLS, A_Q_COLS + 2 * A_KV_COLS,
            A_Q_COLS + 2 * A_KV_COLS + B_QK_COLS, A_Q_COLS + 2 * A_KV_COLS + 2 * B_QK_COLS]
    qa, ka, va, qb, kb, vb = jnp.split(p, offs, axis=-1)
    qa = qa.reshape(b, t, GQA_HEADS, HEAD_DIM)
    ka = ka.reshape(b, t, GQA_KV_HEADS, HEAD_DIM)
    va = va.reshape(b, t, GQA_KV_HEADS, HEAD_DIM)
    qb = qb.reshape(b, t, DIFF_HEADS, 2, HEAD_DIM)
    kb = kb.reshape(b, t, DIFF_HEADS, 2, HEAD_DIM)
    vb = vb.reshape(b, t, DIFF_HEADS, DIFF_V_DIM)
    return qa, ka, va, qb, kb, vb


def rope_diff(x, cos, sin):
    b, t, h, two, d = x.shape
    return apply_axial_rope(x.reshape(b, t, h * two, d), cos, sin).reshape(x.shape)


def gqa_attention(q, k, v, q_block):
    b, t, _, d = q.shape
    nb = t // q_block
    qb = q.reshape(b, nb, q_block, GQA_KV_HEADS, GQA_GROUP, d).transpose(1, 0, 3, 4, 2, 5)
    scale = d ** -0.5

    def block(qi):
        s = jnp.einsum('bhgqd,bkhd->bhgqk', qi, k, preferred_element_type=jnp.float32) * scale
        p = jax.nn.softmax(s, axis=-1)
        return jnp.einsum('bhgqk,bkhd->bqhgd', p.astype(v.dtype), v)

    o = lax.map(block, qb)
    return o.transpose(1, 0, 2, 3, 4, 5).reshape(b, t, GQA_HEADS * d)


def diff_attention(q, k, v, lam, sub_g, lambda_init, q_block):
    b, t, h, _, d = q.shape
    nb = t // q_block
    qb = q.reshape(b, nb, q_block, h, 2, d).transpose(1, 0, 3, 4, 2, 5)
    scale = d ** -0.5

    def block(qi):
        s = jnp.einsum('bhcqd,bkhcd->bhcqk', qi, k, preferred_element_type=jnp.float32) * scale
        p = jax.nn.softmax(s, axis=-1)
        a = p[:, :, 0] - lam * p[:, :, 1]
        return jnp.einsum('bhqk,bkhe->bqhe', a.astype(v.dtype), v)

    o = lax.map(block, qb).transpose(1, 0, 2, 3, 4).reshape(b, t, h, DIFF_V_DIM)
    o = rms_norm(o, sub_g) * (1.0 - lambda_init)
    return o.reshape(b, t, h * DIFF_V_DIM)


def moe_ffn(h, w_router, b_router, w_in, b_in, w_out, b_out):
    b, t, dm = h.shape
    n = b * t
    xt = h.reshape(n, dm)
    logits = (xt @ w_router + b_router).astype(jnp.float32)
    top_val, top_idx = lax.top_k(logits, TOP_K)
    gates = jax.nn.softmax(top_val, axis=-1)
    n_assign = n * TOP_K
    exp_flat = top_idx.reshape(-1).astype(jnp.int32)
    tok_flat = jnp.repeat(jnp.arange(n, dtype=jnp.int32), TOP_K)
    gate_flat = gates.reshape(-1)
    order = jnp.argsort(exp_flat)
    exp_sorted = exp_flat[order]
    counts = jax.ops.segment_sum(jnp.ones_like(exp_flat), exp_flat, num_segments=N_EXPERTS)
    starts = jnp.cumsum(counts) - counts
    padded = (counts + EXPERT_BLOCK - 1) // EXPERT_BLOCK * EXPERT_BLOCK
    pad_ends = jnp.cumsum(padded)
    pad_starts = pad_ends - padded
    dest = pad_starts[exp_sorted] + (jnp.arange(n_assign, dtype=jnp.int32) - starts[exp_sorted])
    n_blocks = -(-n_assign // EXPERT_BLOCK) + N_EXPERTS
    n_rows = n_blocks * EXPERT_BLOCK
    row_tok = jnp.full((n_rows,), n, jnp.int32).at[dest].set(tok_flat[order])
    row_gate = jnp.zeros((n_rows,), jnp.float32).at[dest].set(gate_flat[order])
    block_start = jnp.arange(n_blocks, dtype=jnp.int32) * EXPERT_BLOCK
    block_exp = jnp.minimum(jnp.searchsorted(pad_ends, block_start, side='right'), N_EXPERTS - 1)
    x_pad = jnp.concatenate([xt, jnp.zeros((1, dm), xt.dtype)], axis=0)
    xs = x_pad[row_tok].reshape(n_blocks, EXPERT_BLOCK, dm)

    def expert_block(args):
        xb, e = args
        hcat = xb @ w_in[e] + b_in[e]
        x_glu = jnp.minimum(hcat[:, :D_FF], SWIGLU_LIMIT)
        x_lin = jnp.clip(hcat[:, D_FF:], -SWIGLU_LIMIT, SWIGLU_LIMIT)
        act = x_glu * jax.nn.sigmoid(SWIGLU_ALPHA * x_glu) * (x_lin + 1)
        return act @ w_out[e] + b_out[e]

    ys = lax.map(expert_block, (xs, block_exp)).reshape(n_rows, dm)
    y = jax.ops.segment_sum(ys * row_gate[:, None].astype(ys.dtype), row_tok, num_segments=n + 1)[:n]
    return y.reshape(b, t, dm)


def setup_inputs(seed: int = 0) -> dict:
    key = jax.random.key(seed)
    ks = jax.random.split(key, 21)
    f32 = jnp.float32
    nrm = lambda k, shape, s: jax.random.normal(k, shape, f32) * s
    return {
        'x': nrm(ks[0], (BATCH, SEQ, D_MODEL), 1.0),
        'c': nrm(ks[1], (BATCH, D_MODEL), 1.0),
        'ctx': nrm(ks[2], (BATCH, CTX_LEN, D_MODEL), 1.0),
        'c_ctx': nrm(ks[3], (D_MODEL,), 1.0),
        'w_ada': nrm(ks[4], (DEPTH, D_MODEL, N_MOD * D_MODEL), 0.5 * D_MODEL ** -0.5),
        'b_ada': nrm(ks[5], (DEPTH, N_MOD * D_MODEL), 0.02),
        'g_attn': 1.0 + nrm(ks[6], (DEPTH, D_MODEL), 0.05),
        'w_qkv': nrm(ks[7], (DEPTH, D_MODEL, QKV_COLS), D_MODEL ** -0.5),
        'gqa_q_norm': 1.0 + nrm(ks[8], (DEPTH, HEAD_DIM), 0.05),
        'gqa_k_norm': 1.0 + nrm(ks[9], (DEPTH, HEAD_DIM), 0.05),
        'diff_lambda': nrm(ks[10], (DEPTH, 4, HEAD_DIM), 0.1),
        'diff_subln': 1.0 + nrm(ks[11], (DEPTH, DIFF_V_DIM), 0.05),
        'w_o': nrm(ks[12], (DEPTH, MIX_WIDTH, D_MODEL), MIX_WIDTH ** -0.5),
        'g_ffn': 1.0 + nrm(ks[13], (DEPTH, D_MODEL), 0.05),
        'w_router': nrm(ks[14], (DEPTH, D_MODEL, N_EXPERTS), D_MODEL ** -0.5),
        'b_router': nrm(ks[15], (DEPTH, N_EXPERTS), 0.01),
        'w_in': nrm(ks[16], (DEPTH, N_EXPERTS, D_MODEL, 2 * D_FF), D_MODEL ** -0.5),
        'b_in': nrm(ks[17], (DEPTH, N_EXPERTS, 2 * D_FF), 0.02),
        'w_out': nrm(ks[18], (DEPTH, N_EXPERTS, D_FF, D_MODEL), D_FF ** -0.5),
        'b_out': nrm(ks[19], (DEPTH, N_EXPERTS, D_MODEL), 0.02),
        'g_final': 1.0 + nrm(ks[20], (D_MODEL,), 0.05),
    }


def reference(x, c, ctx, c_ctx, w_ada, b_ada, g_attn, w_qkv, gqa_q_norm, gqa_k_norm, diff_lambda,
              diff_subln, w_o, g_ffn, w_router, b_router, w_in, b_in, w_out, b_out, g_final):
    n_tok = x.shape[1]
    cos, sin = axial_rope_tables(n_tok)
    for l in range(DEPTH):
        lambda_init = 0.8 - 0.6 * math.exp(-0.3 * l)
        mod_x = (jax.nn.silu(c) @ w_ada[l] + b_ada[l])[:, None, :]
        mod_c = (jax.nn.silu(c_ctx) @ w_ada[l] + b_ada[l])[None, None, :]
        sh1, sc1, gt1, sh2, sc2, gt2 = jnp.split(mod_x, N_MOD, axis=-1)
        csh1, csc1, cgt1, csh2, csc2, cgt2 = jnp.split(mod_c, N_MOD, axis=-1)

        h = modulate(rms_norm(x, g_attn[l]), sh1, sc1)
        hc = modulate(rms_norm(ctx, g_attn[l]), csh1, csc1)
        qa, ka, va, qb, kb, vb = split_projection(h @ w_qkv[l])
        qa_c, ka_c, va_c, qb_c, kb_c, vb_c = split_projection(hc @ w_qkv[l])
        qa = apply_axial_rope(rms_norm(qa, gqa_q_norm[l]), cos, sin)
        ka = apply_axial_rope(rms_norm(ka, gqa_k_norm[l]), cos, sin)
        ka_c = rms_norm(ka_c, gqa_k_norm[l])
        qb = rope_diff(qb, cos, sin)
        kb = rope_diff(kb, cos, sin)
        lamf = diff_lambda[l].astype(jnp.float32)
        lam = jnp.exp(jnp.sum(lamf[0] * lamf[1])) - jnp.exp(jnp.sum(lamf[2] * lamf[3])) + lambda_init
        ka_all = jnp.concatenate([ka_c, ka], axis=1)
        va_all = jnp.concatenate([va_c, va], axis=1)
        kb_all = jnp.concatenate([kb_c, kb], axis=1)
        vb_all = jnp.concatenate([vb_c, vb], axis=1)
        lat_mix = jnp.concatenate([
            gqa_attention(qa, ka_all, va_all, Q_BLOCK),
            diff_attention(qb, kb_all, vb_all, lam, diff_subln[l], lambda_init, Q_BLOCK),
        ], axis=-1)
        x = x + gt1 * (lat_mix @ w_o[l])
        x = x + gt2 * moe_ffn(modulate(rms_norm(x, g_ffn[l]), sh2, sc2),
                              w_router[l], b_router[l], w_in[l], b_in[l], w_out[l], b_out[l])

        if l < DEPTH - 1:
            qa_c = rms_norm(qa_c, gqa_q_norm[l])
            ctx_len = ctx.shape[1]
            ctx_mix = jnp.concatenate([
                gqa_attention(qa_c, ka_c, va_c, ctx_len),
                diff_attention(qb_c, kb_c, vb_c, lam, diff_subln[l], lambda_init, ctx_len),
            ], axis=-1)
            ctx = ctx + cgt1 * (ctx_mix @ w_o[l])
            ctx = ctx + cgt2 * moe_ffn(modulate(rms_norm(ctx, g_ffn[l]), csh2, csc2),
                                       w_router[l], b_router[l], w_in[l], b_in[l], w_out[l], b_out[l])
    return rms_norm(x, g_final)
```

```python
import functools

import jax
import jax.numpy as jnp
from jax import lax
from jax.experimental import pallas as pl
from jax.experimental.pallas import tpu as pltpu

F32 = jnp.float32
BF16 = jnp.bfloat16

HEAD_DIM = 64
GQA_HEADS = 8
GQA_KV_HEADS = 2
GQA_GROUP = GQA_HEADS // GQA_KV_HEADS
DIFF_HEADS = 4
DIFF_V_DIM = 2 * HEAD_DIM
GRID_W = 64
ROPE_THETA = 10000.0
N_EXPERTS = 32
TOP_K = 4
SWIGLU_ALPHA = 1.702
SWIGLU_LIMIT = 7.0
N_MOD = 6
EPS = 1e-6
LAMBDA_INIT = 0.8 - 0.6
LOG2E = 1.4426950408889634
Q_SCALE = HEAD_DIM ** -0.5 * LOG2E

LANES = 128
KV_COLS = 2 * GQA_KV_HEADS * HEAD_DIM + DIFF_HEADS * 2 * HEAD_DIM + DIFF_HEADS * DIFF_V_DIM
Q_COLS = GQA_HEADS * HEAD_DIM + DIFF_HEADS * 2 * HEAD_DIM

PROJ_TM = 256
GQA_TQ = 128
DIFF_TQ = 256
ATT_TK = 512
OPROJ_TM = 256
ROW_TM = 256
EXPERT_BLK = 256


def _cparams(n_axes, vmem_mb):
    return pltpu.CompilerParams(dimension_semantics=("arbitrary",) * n_axes,
                                vmem_limit_bytes=vmem_mb * 1024 * 1024)


def _mod_kernel(cv_ref, w_ref, b_ref, o_ref):
    cv = cv_ref[...]
    s = cv * (1.0 / (1.0 + jnp.exp(-cv)))
    o_ref[...] = jnp.dot(s, w_ref[...], preferred_element_type=F32,
                         precision=lax.Precision.HIGHEST) + b_ref[...]


def _mod_call(cv, w_ada, b_ada):
    d, n = w_ada.shape
    tn = 1024
    return pl.pallas_call(
        _mod_kernel,
        grid=(n // tn,),
        in_specs=[pl.BlockSpec((8, d), lambda j: (0, 0)),
                  pl.BlockSpec((d, tn), lambda j: (0, j)),
                  pl.BlockSpec((1, tn), lambda j: (0, j))],
        out_specs=pl.BlockSpec((8, tn), lambda j: (0, j)),
        out_shape=jax.ShapeDtypeStruct((8, n), F32),
        compiler_params=_cparams(1, 32),
    )(cv, w_ada, b_ada.reshape(1, n))


def _head_rms(v, gain, bd):
    sq = v * v
    hi = sq.astype(BF16)
    lo = (sq - hi.astype(F32)).astype(BF16)
    ss = (jnp.dot(hi, bd, preferred_element_type=F32) + jnp.dot(lo, bd, preferred_element_type=F32))
    return v * lax.rsqrt(ss * (1.0 / HEAD_DIM) + EPS) * gain


def _rope_chunk(v, cos, sin, first_half):
    partner = jnp.where(first_half, pltpu.roll(v, LANES - 16, 1), pltpu.roll(v, 16, 1))
    return v * cos + partner * sin


def _proj_kernel(x_ref, sh_ref, sc_ref, g_ref, w_ref, cos_ref, sin_ref, gq_ref, gk_ref, bd_ref,
                 *out_refs, rope, with_q):
    xf = x_ref[0]
    ms = jnp.mean(xf * xf, axis=-1, keepdims=True)
    h = xf * lax.rsqrt(ms + EPS) * g_ref[...]
    h = h * (1.0 + sc_ref[0]) + sh_ref[0]
    p = jnp.dot(h.astype(BF16), w_ref[...], preferred_element_type=F32)

    if with_q:
        ka_ref, va_ref, kb_ref, vb_ref, qa_ref, qb_ref = out_refs
    else:
        ka_ref, va_ref, kb_ref, vb_ref = out_refs
    bd = bd_ref[...]
    if rope:
        cos = cos_ref[...]
        sin = sin_ref[...]
        lane = lax.broadcasted_iota(jnp.int32, cos.shape, 1)
        first_half = (lane % 32) < 16

    def chunk(j):
        return p[:, j * LANES:(j + 1) * LANES]

    def maybe_rope(v):
        return _rope_chunk(v, cos, sin, first_half) if rope else v

    def put_heads(ref, j, v):
        ref[0, 2 * j] = v[:, :HEAD_DIM].astype(ref.dtype)
        ref[0, 2 * j + 1] = v[:, HEAD_DIM:].astype(ref.dtype)

    put_heads(ka_ref, 0, maybe_rope(_head_rms(chunk(0), gk_ref[...], bd)))
    put_heads(va_ref, 0, chunk(1))
    for j in range(4):
        put_heads(kb_ref, j, maybe_rope(chunk(2 + j)))
        vb_ref[0, j] = chunk(6 + j).astype(vb_ref.dtype)
    if with_q:
        for j in range(4):
            put_heads(qa_ref, j, maybe_rope(_head_rms(chunk(10 + j), gq_ref[...], bd)) * Q_SCALE)
            put_heads(qb_ref, j, maybe_rope(chunk(14 + j)) * Q_SCALE)


def _proj_call(x, shift, scale, g, w, cos, sin, gq, gk, bd, *, rope, with_q):
    nb, t, d = x.shape
    tm = min(PROJ_TM, t)
    ncols = w.shape[1]
    head = lambda n: pl.BlockSpec((1, n, tm, HEAD_DIM), lambda b, i: (b, 0, i, 0))
    out_specs = [head(2), head(2), head(8), pl.BlockSpec((1, 4, tm, DIFF_V_DIM), lambda b, i: (b, 0, i, 0))]
    out_shape = [jax.ShapeDtypeStruct((nb, 2, t, HEAD_DIM), BF16),
                 jax.ShapeDtypeStruct((nb, 2, t, HEAD_DIM), BF16),
                 jax.ShapeDtypeStruct((nb, 8, t, HEAD_DIM), BF16),
                 jax.ShapeDtypeStruct((nb, 4, t, DIFF_V_DIM), BF16)]
    if with_q:
        out_specs += [head(8), head(8)]
        out_shape += [jax.ShapeDtypeStruct((nb, 8, t, HEAD_DIM), BF16)] * 2
    const = lambda shape: pl.BlockSpec(shape, lambda b, i: (0,) * len(shape))
    return pl.pallas_call(
        functools.partial(_proj_kernel, rope=rope, with_q=with_q),
        grid=(nb, t // tm),
        in_specs=[pl.BlockSpec((1, tm, d), lambda b, i: (b, i, 0)),
                  pl.BlockSpec((1, 1, d), lambda b, i: (b, 0, 0)),
                  pl.BlockSpec((1, 1, d), lambda b, i: (b, 0, 0)),
                  const((1, d)),
                  const((d, ncols)),
                  pl.BlockSpec((tm, LANES), lambda b, i: (i, 0)),
                  pl.BlockSpec((tm, LANES), lambda b, i: (i, 0)),
                  const((1, LANES)), const((1, LANES)), const((LANES, LANES))],
        out_specs=out_specs,
        out_shape=out_shape,
        compiler_params=_cparams(2, 48),
    )(x, shift, scale, g, w, cos, sin, gq, gk, bd)


def _softmax_step(q, k, v, carry):
    m, l, acc = carry
    s = lax.dot_general(q, k, (((1,), (1,)), ((), ())), preferred_element_type=F32)
    m_new = jnp.maximum(m, jnp.max(s, axis=1, keepdims=True))
    alpha = jnp.exp2(m - m_new)
    p = jnp.exp2(s - m_new)
    l_new = alpha * l + jnp.sum(p, axis=1, keepdims=True)
    acc_new = alpha * acc + jnp.dot(p.astype(BF16), v, preferred_element_type=F32)
    return m_new, l_new, acc_new


def _init_carry(m_rows, dv):
    return (jnp.full((m_rows, 1), -jnp.inf, F32), jnp.zeros((m_rows, 1), F32), jnp.zeros((m_rows, dv), F32))


def _gqa_kernel(q_ref, kc_ref, vc_ref, kl_ref, vl_ref, o_ref, *, tk):
    tq = q_ref.shape[2]
    q = q_ref[0].reshape(GQA_GROUP * tq, HEAD_DIM)
    carry = _softmax_step(q, kc_ref[0, 0], vc_ref[0, 0], _init_carry(GQA_GROUP * tq, HEAD_DIM))

    def body(j, carry):
        off = pl.multiple_of(j * tk, tk)
        return _softmax_step(q, kl_ref[0, 0, pl.ds(off, tk), :], vl_ref[0, 0, pl.ds(off, tk), :], carry)

    _, l, acc = lax.fori_loop(0, kl_ref.shape[2] // tk, body, carry)
    o = (acc / l).reshape(GQA_GROUP, tq, HEAD_DIM)
    for g in range(GQA_GROUP):
        o_ref[0, :, g * HEAD_DIM:(g + 1) * HEAD_DIM] = o[g].astype(o_ref.dtype)


def _gqa_call(qa, ka_c, va_c, ka, va):
    b, _, t, _ = qa.shape
    c = ka_c.shape[2]
    tq = min(GQA_TQ, t)
    tk = min(ATT_TK, t)
    kv = lambda n: pl.BlockSpec((1, 1, n, HEAD_DIM), lambda bi, h, i: (bi, h, 0, 0))
    return pl.pallas_call(
        functools.partial(_gqa_kernel, tk=tk),
        grid=(b, GQA_KV_HEADS, t // tq),
        in_specs=[pl.BlockSpec((1, GQA_GROUP, tq, HEAD_DIM), lambda bi, h, i: (bi, h, i, 0)),
                  kv(c), kv(c), kv(t), kv(t)],
        out_specs=pl.BlockSpec((1, tq, GQA_GROUP * HEAD_DIM), lambda bi, h, i: (bi, i, h)),
        out_shape=jax.ShapeDtypeStruct((b, t, GQA_HEADS * HEAD_DIM), BF16),
        compiler_params=_cparams(3, 48),
    )(qa, ka_c, va_c, ka, va)


def _diff_kernel(q_ref, kc_ref, vc_ref, kl_ref, vl_ref, lam_ref, sg_ref, o_ref, *, tk):
    tq = q_ref.shape[2]
    q = (q_ref[0, 0], q_ref[0, 1])
    carry = tuple(_softmax_step(q[c], kc_ref[0, c], vc_ref[0, 0], _init_carry(tq, DIFF_V_DIM)) for c in range(2))

    def body(j, carry):
        off = pl.multiple_of(j * tk, tk)
        v = vl_ref[0, 0, pl.ds(off, tk), :]
        return tuple(_softmax_step(q[c], kl_ref[0, c, pl.ds(off, tk), :], v, carry[c]) for c in range(2))

    (_, l1, a1), (_, l2, a2) = lax.fori_loop(0, kl_ref.shape[2] // tk, body, carry)
    lamf = lam_ref[...]
    lam = (jnp.exp(jnp.sum(lamf[0:1] * lamf[1:2], axis=-1, keepdims=True))
           - jnp.exp(jnp.sum(lamf[2:3] * lamf[3:4], axis=-1, keepdims=True)) + LAMBDA_INIT)
    o = a1 / l1 - lam * (a2 / l2)
    ms = jnp.mean(o * o, axis=-1, keepdims=True)
    o = o * lax.rsqrt(ms + EPS) * sg_ref[...] * (1.0 - LAMBDA_INIT)
    o_ref[0] = o.astype(o_ref.dtype)


def _diff_call(qb, kb_c, vb_c, kb, vb, lam, sub_g):
    b, _, t, _ = qb.shape
    c = kb_c.shape[2]
    tq = min(DIFF_TQ, t)
    tk = min(ATT_TK, t)
    return pl.pallas_call(
        functools.partial(_diff_kernel, tk=tk),
        grid=(b, DIFF_HEADS, t // tq),
        in_specs=[pl.BlockSpec((1, 2, tq, HEAD_DIM), lambda bi, h, i: (bi, h, i, 0)),
                  pl.BlockSpec((1, 2, c, HEAD_DIM), lambda bi, h, i: (bi, h, 0, 0)),
                  pl.BlockSpec((1, 1, c, DIFF_V_DIM), lambda bi, h, i: (bi, h, 0, 0)),
                  pl.BlockSpec((1, 2, t, HEAD_DIM), lambda bi, h, i: (bi, h, 0, 0)),
                  pl.BlockSpec((1, 1, t, DIFF_V_DIM), lambda bi, h, i: (bi, h, 0, 0)),
                  pl.BlockSpec((4, HEAD_DIM), lambda bi, h, i: (0, 0)),
                  pl.BlockSpec((1, DIFF_V_DIM), lambda bi, h, i: (0, 0))],
        out_specs=pl.BlockSpec((1, tq, DIFF_V_DIM), lambda bi, h, i: (bi, i, h)),
        out_shape=jax.ShapeDtypeStruct((b, t, DIFF_HEADS * DIFF_V_DIM), BF16),
        compiler_params=_cparams(3, 48),
    )(qb, kb_c, vb_c, kb, vb, lam, sub_g)


def _oproj_kernel(a_ref, d_ref, x_ref, gt_ref, sh_ref, sc_ref, g_ref, wo_ref, wr_ref, br_ref, tri_ref,
                  x1_ref, h2_ref, idx_ref, rank_ref, gate_ref, cnt_ref):
    i = pl.program_id(0)

    @pl.when(i == 0)
    def _():
        cnt_ref[...] = jnp.zeros_like(cnt_ref)

    half = a_ref.shape[1]
    mix = (jnp.dot(a_ref[...], wo_ref[0:half, :], preferred_element_type=F32)
           + jnp.dot(d_ref[...], wo_ref[half:2 * half, :], preferred_element_type=F32))
    x1 = x_ref[...] + gt_ref[0] * mix
    x1_ref[...] = x1
    ms = jnp.mean(x1 * x1, axis=-1, keepdims=True)
    h2 = x1 * lax.rsqrt(ms + EPS) * g_ref[...]
    h2 = h2 * (1.0 + sc_ref[0]) + sh_ref[0]
    h2_ref[...] = h2

    logits = jnp.dot(h2, wr_ref[...], preferred_element_type=F32,
                     precision=lax.Precision.HIGHEST) + br_ref[...]
    lt = logits.T[0:N_EXPERTS, :]
    tm = lt.shape[1]
    eid = lax.broadcasted_iota(jnp.int32, lt.shape, 0).astype(F32)
    vals, sels = [], []
    for k in range(TOP_K):
        m = jnp.max(lt, axis=0, keepdims=True)
        first = jnp.min(jnp.where(lt == m, eid, float(N_EXPERTS)), axis=0, keepdims=True)
        sel = eid == first
        lt = jnp.where(sel, -jnp.inf, lt)
        vals.append(m)
        sels.append(sel)
        idx_ref[k:k + 1, :] = first.astype(jnp.int32)

    onehot = sum(s.astype(F32) for s in sels)
    before = cnt_ref[:, 0:1] + jnp.dot(onehot.astype(BF16), tri_ref[...], preferred_element_type=F32)
    for k in range(TOP_K):
        rank_ref[k:k + 1, :] = jnp.sum(jnp.where(sels[k], before, 0.0), axis=0, keepdims=True).astype(jnp.int32)
    cnt_ref[...] = cnt_ref[...] + jnp.sum(onehot, axis=1, keepdims=True)

    es = [jnp.exp(v - vals[0]) for v in vals]
    den = es[0] + es[1] + es[2] + es[3]
    gates = jnp.concatenate([e / den for e in es] + [jnp.zeros((LANES - TOP_K, tm), F32)], axis=0)
    gate_ref[...] = gates.T


def _oproj_call(a, d, x, gt1, sh2, sc2, g_ffn, w_o, wr_pad, br_pad, tiles_per_batch):
    n, dm = x.shape
    tm = OPROJ_TM
    tri = (lax.broadcasted_iota(jnp.int32, (tm, tm), 0) < lax.broadcasted_iota(jnp.int32, (tm, tm), 1)).astype(BF16)
    row = lambda w: pl.BlockSpec((tm, w), lambda i: (i, 0))
    per_batch = pl.BlockSpec((1, 1, dm), lambda i: (i // tiles_per_batch, 0, 0))
    const = lambda shape: pl.BlockSpec(shape, lambda i: (0,) * len(shape))
    return pl.pallas_call(
        _oproj_kernel,
        grid=(n // tm,),
        in_specs=[row(a.shape[1]), row(d.shape[1]), row(dm), per_batch, per_batch, per_batch,
                  const((1, dm)), const(w_o.shape), const(wr_pad.shape), const((1, LANES)), const((tm, tm))],
        out_specs=[row(dm), row(dm),
                   pl.BlockSpec((TOP_K, tm), lambda i: (0, i)),
                   pl.BlockSpec((TOP_K, tm), lambda i: (0, i)),
                   row(LANES),
                   const((N_EXPERTS, LANES))],
        out_shape=[jax.ShapeDtypeStruct((n, dm), F32),
                   jax.ShapeDtypeStruct((n, dm), F32),
                   jax.ShapeDtypeStruct((TOP_K, n), jnp.int32),
                   jax.ShapeDtypeStruct((TOP_K, n), jnp.int32),
                   jax.ShapeDtypeStruct((n, LANES), F32),
                   jax.ShapeDtypeStruct((N_EXPERTS, LANES), F32)],
        compiler_params=_cparams(1, 48),
    )(a, d, x, gt1, sh2, sc2, g_ffn, w_o, wr_pad, br_pad, tri)


def _row_copy(src_hbm, src_row, dst_ref, dst_row, sem):
    return pltpu.make_async_copy(src_hbm.at[pl.ds(src_row, 1)], dst_ref.at[pl.ds(dst_row, 1)], sem)


def _dispatch_kernel(cnt_ref, pad_ref, start_ref, nused_ref, dest_hbm, h2_hbm, zero_hbm, xs_hbm,
                     idx_smem, sem_i, sem_d, sem_z, *, tm, blk):
    i = pl.program_id(0)

    @pl.when(i == 0)
    def _():
        for e in range(N_EXPERTS):
            lo = start_ref[e] + cnt_ref[e]
            hi = start_ref[e] + pad_ref[e]
            lax.fori_loop(lo, hi, lambda r, c: (_row_copy(zero_hbm, 0, xs_hbm, r, sem_d).start(), c)[1], 0)
            lax.fori_loop(lo, hi, lambda r, c: (_row_copy(zero_hbm, 0, xs_hbm, r, sem_d).wait(), c)[1], 0)

        def tail_copy(j):
            return pltpu.make_async_copy(zero_hbm, xs_hbm.at[pl.ds(pl.multiple_of(j * blk, blk), blk)], sem_z)

        n_blocks = xs_hbm.shape[0] // blk
        lax.fori_loop(nused_ref[0], n_blocks, lambda j, c: (tail_copy(j).start(), c)[1], 0)
        lax.fori_loop(nused_ref[0], n_blocks, lambda j, c: (tail_copy(j).wait(), c)[1], 0)

    idx_copy = pltpu.make_async_copy(dest_hbm.at[pl.ds(i * (TOP_K * tm), TOP_K * tm)], idx_smem, sem_i)
    idx_copy.start()
    idx_copy.wait()
    base = i * tm

    def issue(t, c):
        for k in range(TOP_K):
            _row_copy(h2_hbm, base + t, xs_hbm, idx_smem[k * tm + t], sem_d).start()
        return c

    def drain(t, c):
        for k in range(TOP_K):
            _row_copy(h2_hbm, 0, xs_hbm, 0, sem_d).wait()
        return c

    lax.fori_loop(0, tm, issue, 0)
    lax.fori_loop(0, tm, drain, 0)


def _dispatch_call(counts, padded, starts, n_used, dest_flat, h2, n_rows):
    n, dm = h2.shape
    tm = ROW_TM
    any_spec = pl.BlockSpec(memory_space=pl.ANY)
    return pl.pallas_call(
        functools.partial(_dispatch_kernel, tm=tm, blk=EXPERT_BLK),
        grid_spec=pltpu.PrefetchScalarGridSpec(
            num_scalar_prefetch=4,
            grid=(n // tm,),
            in_specs=[any_spec, any_spec, any_spec],
            out_specs=any_spec,
            scratch_shapes=[pltpu.SMEM((TOP_K * tm,), jnp.int32),
                            pltpu.SemaphoreType.DMA(()), pltpu.SemaphoreType.DMA(()),
                            pltpu.SemaphoreType.DMA(())]),
        out_shape=jax.ShapeDtypeStruct((n_rows, dm), h2.dtype),
        compiler_params=_cparams(1, 16),
    )(counts, padded, starts, n_used, dest_flat, h2, jnp.zeros((EXPERT_BLK, dm), h2.dtype))


def _experts_kernel(bexp_ref, nused_ref, xs_ref, win_ref, bin_ref, wout_ref, bout_ref, ys_ref, win_bf, wout_bf):
    i = pl.program_id(0)
    prev = bexp_ref[jnp.maximum(i - 1, 0)]
    new_expert = jnp.logical_or(i == 0, bexp_ref[i] != prev)

    @pl.when(jnp.logical_and(i < nused_ref[0], new_expert))
    def _():
        win_bf[...] = win_ref[0].astype(BF16)
        wout_bf[...] = wout_ref[0].astype(BF16)

    @pl.when(i < nused_ref[0])
    def _():
        dff = wout_bf.shape[0]
        hcat = jnp.dot(xs_ref[...].astype(BF16), win_bf[...], preferred_element_type=F32) + bin_ref[0]
        x_glu = jnp.minimum(hcat[:, :dff], SWIGLU_LIMIT)
        x_lin = jnp.clip(hcat[:, dff:], -SWIGLU_LIMIT, SWIGLU_LIMIT)
        act = x_glu * (1.0 / (1.0 + jnp.exp(-SWIGLU_ALPHA * x_glu))) * (x_lin + 1.0)
        ys_ref[...] = jnp.dot(act.astype(BF16), wout_bf[...], preferred_element_type=F32) + bout_ref[0]

    @pl.when(i >= nused_ref[0])
    def _():
        ys_ref[...] = jnp.zeros_like(ys_ref)


def _experts_call(block_exp, n_used, xs, w_in, b_in, w_out, b_out):
    n_rows, dm = xs.shape
    ne, _, dff2 = w_in.shape
    dff = dff2 // 2
    blk = EXPERT_BLK
    rows = pl.BlockSpec((blk, dm), lambda i, be, nu: (jnp.minimum(i, nu[0] - 1), 0))
    return pl.pallas_call(
        _experts_kernel,
        grid_spec=pltpu.PrefetchScalarGridSpec(
            num_scalar_prefetch=2,
            grid=(n_rows // blk,),
            in_specs=[rows,
                      pl.BlockSpec((1, dm, dff2), lambda i, be, nu: (be[i], 0, 0)),
                      pl.BlockSpec((1, 1, dff2), lambda i, be, nu: (be[i], 0, 0)),
                      pl.BlockSpec((1, dff, dm), lambda i, be, nu: (be[i], 0, 0)),
                      pl.BlockSpec((1, 1, dm), lambda i, be, nu: (be[i], 0, 0))],
            out_specs=pl.BlockSpec((blk, dm), lambda i, be, nu: (i, 0)),
            scratch_shapes=[pltpu.VMEM((dm, dff2), BF16), pltpu.VMEM((dff, dm), BF16)]),
        out_shape=jax.ShapeDtypeStruct((n_rows, dm), F32),
        compiler_params=_cparams(1, 56),
    )(block_exp, n_used, xs, w_in, b_in.reshape(ne, 1, dff2), w_out, b_out.reshape(ne, 1, dm))


def _combine_kernel(dest_hbm, ys_hbm, x1_ref, gate_ref, gt_ref, gf_ref, o_ref, idx_smem, rows, sem_i, sem_d, *, tm):
    i = pl.program_id(0)
    idx_copy = pltpu.make_async_copy(dest_hbm.at[pl.ds(i * (TOP_K * tm), TOP_K * tm)], idx_smem, sem_i)
    idx_copy.start()
    idx_copy.wait()

    def issue(t, c):
        for k in range(TOP_K):
            _row_copy(ys_hbm, idx_smem[k * tm + t], rows.at[k], t, sem_d).start()
        return c

    def drain(t, c):
        for k in range(TOP_K):
            _row_copy(ys_hbm, 0, rows.at[k], 0, sem_d).wait()
        return c

    lax.fori_loop(0, tm, issue, 0)
    lax.fori_loop(0, tm, drain, 0)

    gates = gate_ref[...]
    y = gates[:, 0:1] * rows[0]
    for k in range(1, TOP_K):
        y = y + gates[:, k:k + 1] * rows[k]
    x2 = x1_ref[...] + gt_ref[0] * y
    ms = jnp.mean(x2 * x2, axis=-1, keepdims=True)
    o_ref[...] = x2 * lax.rsqrt(ms + EPS) * gf_ref[...]


def _combine_call(dest_flat, ys, x1, gate_t, gt2, g_final, tiles_per_batch):
    n, dm = x1.shape
    tm = ROW_TM
    row = lambda w: pl.BlockSpec((tm, w), lambda i: (i, 0))
    return pl.pallas_call(
        functools.partial(_combine_kernel, tm=tm),
        grid=(n // tm,),
        in_specs=[pl.BlockSpec(memory_space=pl.ANY), pl.BlockSpec(memory_space=pl.ANY),
                  row(dm), row(LANES),
                  pl.BlockSpec((1, 1, dm), lambda i: (i // tiles_per_batch, 0, 0)),
                  pl.BlockSpec((1, dm), lambda i: (0, 0))],
        out_specs=row(dm),
        out_shape=jax.ShapeDtypeStruct((n, dm), F32),
        scratch_shapes=[pltpu.SMEM((TOP_K * tm,), jnp.int32),
                        pltpu.VMEM((TOP_K, tm, dm), F32),
                        pltpu.SemaphoreType.DMA(()), pltpu.SemaphoreType.DMA(())],
        compiler_params=_cparams(1, 32),
    )(dest_flat, ys, x1, gate_t, gt2, g_final)


def _rope_tables(t):
    tok = jnp.arange(t, dtype=jnp.int32)
    row = (tok // GRID_W).astype(F32)
    col = (tok % GRID_W).astype(F32)
    n_freq = HEAD_DIM // 4
    inv_freq = ROPE_THETA ** (-jnp.arange(n_freq, dtype=F32) / n_freq)
    ar = row[:, None] * inv_freq
    ac = col[:, None] * inv_freq
    cos = jnp.concatenate([jnp.cos(ar), jnp.cos(ar), jnp.cos(ac), jnp.cos(ac)], axis=1)
    sin = jnp.concatenate([-jnp.sin(ar), jnp.sin(ar), -jnp.sin(ac), jnp.sin(ac)], axis=1)
    return jnp.tile(cos, (1, 2)), jnp.tile(sin, (1, 2))


def kernel(x, c, ctx, c_ctx, w_ada, b_ada, g_attn, w_qkv, gqa_q_norm, gqa_k_norm, diff_lambda,
           diff_subln, w_o, g_ffn, w_router, b_router, w_in, b_in, w_out, b_out, g_final):
    assert w_ada.shape[0] == 1, "single-layer block"
    b, t, dm = x.shape
    n = b * t

    cv = jnp.zeros((8, dm), F32).at[:b].set(c).at[b].set(c_ctx)
    mod = _mod_call(cv, w_ada[0], b_ada[0])
    sh1, sc1, gt1, sh2, sc2, gt2 = [m[:b, None, :] for m in jnp.split(mod, N_MOD, axis=-1)]
    csh1, csc1 = [jnp.broadcast_to(m[b][None, None, :], (b, 1, dm)) for m in jnp.split(mod, N_MOD, axis=-1)[:2]]

    wq = w_qkv[0]
    o_ka, o_va, o_qb, o_kb, o_vb = 512, 640, 768, 1280, 1792
    w_kv = jnp.concatenate([wq[:, o_ka:o_qb], wq[:, o_kb:]], axis=1)
    w_all = jnp.concatenate([w_kv, wq[:, :o_ka], wq[:, o_qb:o_kb]], axis=1).astype(BF16)
    gq = jnp.tile(gqa_q_norm[0], 2)[None, :]
    gk = jnp.tile(gqa_k_norm[0], 2)[None, :]
    lane = jnp.arange(LANES)
    bd = (lane[:, None] // HEAD_DIM == lane[None, :] // HEAD_DIM).astype(BF16)
    cos, sin = _rope_tables(t)
    g_a = g_attn[0][None, :]

    ka, va, kb, vb, qa, qb = _proj_call(x, sh1, sc1, g_a, w_all, cos, sin, gq, gk, bd, rope=True, with_q=True)
    ka_c, va_c, kb_c, vb_c = _proj_call(ctx, csh1, csc1, g_a, w_all[:, :KV_COLS], cos, sin, gq, gk, bd,
                                        rope=False, with_q=False)

    att_a = _gqa_call(qa, ka_c, va_c, ka, va)
    att_d = _diff_call(qb, kb_c, vb_c, kb, vb, diff_lambda[0], diff_subln[0][None, :])

    wr_pad = jnp.zeros((dm, LANES), F32).at[:, :N_EXPERTS].set(w_router[0])
    br_pad = jnp.zeros((1, LANES), F32).at[0, :N_EXPERTS].set(b_router[0])
    x1, h2, idx, rank, gate_t, cnt = _oproj_call(
        att_a.reshape(n, -1), att_d.reshape(n, -1), x.reshape(n, dm), gt1, sh2, sc2, g_ffn[0][None, :],
        w_o[0].astype(BF16), wr_pad, br_pad, t // OPROJ_TM)

    blk = EXPERT_BLK
    counts = cnt[:, 0].astype(jnp.int32)
    padded = (counts + blk - 1) // blk * blk
    pad_ends = jnp.cumsum(padded)
    starts = pad_ends - padded
    n_blocks = n * TOP_K // blk + N_EXPERTS
    dest = starts[idx] + rank
    dest_flat = dest.reshape(TOP_K, n // ROW_TM, ROW_TM).transpose(1, 0, 2).reshape(-1)
    block_exp = jnp.minimum(jnp.searchsorted(pad_ends, jnp.arange(n_blocks, dtype=jnp.int32) * blk, side='right'),
                            N_EXPERTS - 1).astype(jnp.int32)
    n_used = (pad_ends[-1:] // blk).astype(jnp.int32)

    xs = _dispatch_call(counts, padded, starts, n_used, dest_flat, h2, n_blocks * blk)
    ys = _experts_call(block_exp, n_used, xs, w_in[0], b_in[0], w_out[0], b_out[0])
    out = _combine_call(dest_flat, ys, x1, gate_t, gt2, g_final[None, :], t // ROW_TM)
    return out.reshape(b, t, dm)
```

```python
import functools

import jax
import jax.numpy as jnp
from jax import lax
from jax.experimental import pallas as pl
from jax.experimental.pallas import tpu as pltpu

F32 = jnp.float32
BF16 = jnp.bfloat16

HEAD_DIM = 64
GQA_HEADS = 8
GQA_KV_HEADS = 2
GQA_GROUP = GQA_HEADS // GQA_KV_HEADS
DIFF_HEADS = 4
DIFF_V_DIM = 2 * HEAD_DIM
GRID_W = 64
ROPE_THETA = 10000.0
N_EXPERTS = 32
TOP_K = 4
SWIGLU_ALPHA = 1.702
SWIGLU_LIMIT = 7.0
N_MOD = 6
EPS = 1e-6
LAMBDA_INIT = 0.8 - 0.6
LOG2E = 1.4426950408889634
Q_SCALE = HEAD_DIM ** -0.5 * LOG2E

LANES = 128
KV_COLS = 2 * GQA_KV_HEADS * HEAD_DIM + DIFF_HEADS * 2 * HEAD_DIM + DIFF_HEADS * DIFF_V_DIM
Q_COLS = GQA_HEADS * HEAD_DIM + DIFF_HEADS * 2 * HEAD_DIM

PROJ_TM = 256
GQA_TQ = 128
DIFF_TQ = 256
ATT_TK = 512
OPROJ_TM = 256
ROW_TM = 256
EXPERT_BLK = 256


def _cparams(n_axes, vmem_mb):
    return pltpu.CompilerParams(dimension_semantics=("arbitrary",) * n_axes,
                                vmem_limit_bytes=vmem_mb * 1024 * 1024)


def _mod_kernel(cv_ref, w_ref, b_ref, o_ref):
    cv = cv_ref[...]
    s = cv * (1.0 / (1.0 + jnp.exp(-cv)))
    o_ref[...] = jnp.dot(s, w_ref[...], preferred_element_type=F32,
                         precision=lax.Precision.HIGHEST) + b_ref[...]


def _mod_call(cv, w_ada, b_ada):
    d, n = w_ada.shape
    tn = 1024
    return pl.pallas_call(
        _mod_kernel,
        grid=(n // tn,),
        in_specs=[pl.BlockSpec((8, d), lambda j: (0, 0)),
                  pl.BlockSpec((d, tn), lambda j: (0, j)),
                  pl.BlockSpec((1, tn), lambda j: (0, j))],
        out_specs=pl.BlockSpec((8, tn), lambda j: (0, j)),
        out_shape=jax.ShapeDtypeStruct((8, n), F32),
        compiler_params=_cparams(1, 32),
    )(cv, w_ada, b_ada.reshape(1, n))


def _head_rms(v, gain, bd):
    sq = v * v
    hi = sq.astype(BF16)
    lo = (sq - hi.astype(F32)).astype(BF16)
    ss = (jnp.dot(hi, bd, preferred_element_type=F32) + jnp.dot(lo, bd, preferred_element_type=F32))
    return v * lax.rsqrt(ss * (1.0 / HEAD_DIM) + EPS) * gain


def _rope_chunk(v, cos, sin, first_half):
    partner = jnp.where(first_half, pltpu.roll(v, LANES - 16, 1), pltpu.roll(v, 16, 1))
    return v * cos + partner * sin


def _proj_kernel(x_ref, sh_ref, sc_ref, g_ref, w_ref, cos_ref, sin_ref, gq_ref, gk_ref, bd_ref,
                 *out_refs, rope, with_q):
    xf = x_ref[0]
    ms = jnp.mean(xf * xf, axis=-1, keepdims=True)
    h = xf * lax.rsqrt(ms + EPS) * g_ref[...]
    h = h * (1.0 + sc_ref[0]) + sh_ref[0]
    p = jnp.dot(h.astype(BF16), w_ref[...], preferred_element_type=F32)

    if with_q:
        ka_ref, va_ref, kb_ref, vb_ref, qa_ref, qb_ref = out_refs
    else:
        ka_ref, va_ref, kb_ref, vb_ref = out_refs
    bd = bd_ref[...]
    if rope:
        cos = cos_ref[...]
        sin = sin_ref[...]
        lane = lax.broadcasted_iota(jnp.int32, cos.shape, 1)
        first_half = (lane % 32) < 16

    def chunk(j):
        return p[:, j * LANES:(j + 1) * LANES]

    def maybe_rope(v):
        return _rope_chunk(v, cos, sin, first_half) if rope else v

    def put_heads(ref, j, v):
        ref[0, 2 * j] = v[:, :HEAD_DIM].astype(ref.dtype)
        ref[0, 2 * j + 1] = v[:, HEAD_DIM:].astype(ref.dtype)

    put_heads(ka_ref, 0, maybe_rope(_head_rms(chunk(0), gk_ref[...], bd)))
    put_heads(va_ref, 0, chunk(1))
    for j in range(4):
        put_heads(kb_ref, j, maybe_rope(chunk(2 + j)))
        vb_ref[0, j] = chunk(6 + j).astype(vb_ref.dtype)
    if with_q:
        for j in range(4):
            put_heads(qa_ref, j, maybe_rope(_head_rms(chunk(10 + j), gq_ref[...], bd)) * Q_SCALE)
            put_heads(qb_ref, j, maybe_rope(chunk(14 + j)) * Q_SCALE)


def _proj_call(x, shift, scale, g, w, cos, sin, gq, gk, bd, *, rope, with_q):
    nb, t, d = x.shape
    tm = min(PROJ_TM, t)
    ncols = w.shape[1]
    head = lambda n: pl.BlockSpec((1, n, tm, HEAD_DIM), lambda b, i: (b, 0, i, 0))
    out_specs = [head(2), head(2), head(8), pl.BlockSpec((1, 4, tm, DIFF_V_DIM), lambda b, i: (b, 0, i, 0))]
    out_shape = [jax.ShapeDtypeStruct((nb, 2, t, HEAD_DIM), BF16),
                 jax.ShapeDtypeStruct((nb, 2, t, HEAD_DIM), BF16),
                 jax.ShapeDtypeStruct((nb, 8, t, HEAD_DIM), BF16),
                 jax.ShapeDtypeStruct((nb, 4, t, DIFF_V_DIM), BF16)]
    if with_q:
        out_specs += [head(8), head(8)]
        out_shape += [jax.ShapeDtypeStruct((nb, 8, t, HEAD_DIM), BF16)] * 2
    const = lambda shape: pl.BlockSpec(shape, lambda b, i: (0,) * len(shape))
    return pl.pallas_call(
        functools.partial(_proj_kernel, rope=rope, with_q=with_q),
        grid=(nb, t // tm),
        in_specs=[pl.BlockSpec((1, tm, d), lambda b, i: (b, i, 0)),
                  pl.BlockSpec((1, 1, d), lambda b, i: (b, 0, 0)),
                  pl.BlockSpec((1, 1, d), lambda b, i: (b, 0, 0)),
                  const((1, d)),
                  const((d, ncols)),
                  pl.BlockSpec((tm, LANES), lambda b, i: (i, 0)),
                  pl.BlockSpec((tm, LANES), lambda b, i: (i, 0)),
                  const((1, LANES)), const((1, LANES)), const((LANES, LANES))],
        out_specs=out_specs,
        out_shape=out_shape,
        compiler_params=_cparams(2, 48),
    )(x, shift, scale, g, w, cos, sin, gq, gk, bd)


def _scores(q, k):
    return lax.dot_general(q, k, (((1,), (1,)), ((), ())), preferred_element_type=F32)


def _softmax_step(s, v, carry):
    m, l, acc = carry
    m_new = jnp.maximum(m, jnp.max(s, axis=1, keepdims=True))
    alpha = jnp.exp2(m - m_new)
    p = jnp.exp2(s - m_new)
    l_new = alpha * l + jnp.sum(p, axis=1, keepdims=True)
    acc_new = alpha * acc + jnp.dot(p.astype(BF16), v, preferred_element_type=F32)
    return m_new, l_new, acc_new


def _init_carry(m_rows, dv):
    return (jnp.full((m_rows, 1), -jnp.inf, F32), jnp.zeros((m_rows, 1), F32), jnp.zeros((m_rows, dv), F32))


def _gqa_kernel(q_ref, kc_ref, vc_ref, kl_ref, vl_ref, o_ref, *, tk):
    tq = q_ref.shape[2]
    q = q_ref[0].reshape(GQA_GROUP * tq, HEAD_DIM)
    carry = _softmax_step(_scores(q, kc_ref[0, 0]), vc_ref[0, 0], _init_carry(GQA_GROUP * tq, HEAD_DIM))

    def body(j, carry):
        rows = pl.ds(pl.multiple_of(j * tk, tk), tk)
        return _softmax_step(_scores(q, kl_ref[0, 0, rows, :]), vl_ref[0, 0, rows, :], carry)

    _, l, acc = lax.fori_loop(0, kl_ref.shape[2] // tk, body, carry, unroll=True)
    o = (acc / l).reshape(GQA_GROUP, tq, HEAD_DIM)
    for g in range(GQA_GROUP):
        o_ref[0, :, g * HEAD_DIM:(g + 1) * HEAD_DIM] = o[g].astype(o_ref.dtype)


def _gqa_call(qa, ka_c, va_c, ka, va):
    b, _, t, _ = qa.shape
    c = ka_c.shape[2]
    tq = min(GQA_TQ, t)
    tk = min(ATT_TK, t)
    kv = lambda n: pl.BlockSpec((1, 1, n, HEAD_DIM), lambda bi, h, i: (bi, h, 0, 0))
    return pl.pallas_call(
        functools.partial(_gqa_kernel, tk=tk),
        grid=(b, GQA_KV_HEADS, t // tq),
        in_specs=[pl.BlockSpec((1, GQA_GROUP, tq, HEAD_DIM), lambda bi, h, i: (bi, h, i, 0)),
                  kv(c), kv(c), kv(t), kv(t)],
        out_specs=pl.BlockSpec((1, tq, GQA_GROUP * HEAD_DIM), lambda bi, h, i: (bi, i, h)),
        out_shape=jax.ShapeDtypeStruct((b, t, GQA_HEADS * HEAD_DIM), BF16),
        compiler_params=_cparams(3, 48),
    )(qa, ka_c, va_c, ka, va)


def _diff_kernel(q_ref, kc_ref, vc_ref, kl_ref, vl_ref, lam_ref, sg_ref, o_ref, *, tk):
    tq = q_ref.shape[2]
    q = (q_ref[0, 0], q_ref[0, 1])

    def scores(k_ref, rows):
        return jnp.concatenate([_scores(q[c], k_ref[0, c, rows, :]) for c in range(2)], axis=0)

    carry = _softmax_step(scores(kc_ref, slice(None)), vc_ref[0, 0], _init_carry(2 * tq, DIFF_V_DIM))

    def body(j, carry):
        rows = pl.ds(pl.multiple_of(j * tk, tk), tk)
        return _softmax_step(scores(kl_ref, rows), vl_ref[0, 0, rows, :], carry)

    _, l, acc = lax.fori_loop(0, kl_ref.shape[2] // tk, body, carry, unroll=True)
    o_both = acc / l
    lamf = lam_ref[...]
    lam = (jnp.exp(jnp.sum(lamf[0:1] * lamf[1:2], axis=-1, keepdims=True))
           - jnp.exp(jnp.sum(lamf[2:3] * lamf[3:4], axis=-1, keepdims=True)) + LAMBDA_INIT)
    o = o_both[:tq] - lam * o_both[tq:]
    ms = jnp.mean(o * o, axis=-1, keepdims=True)
    o = o * lax.rsqrt(ms + EPS) * sg_ref[...] * (1.0 - LAMBDA_INIT)
    o_ref[0] = o.astype(o_ref.dtype)


def _diff_call(qb, kb_c, vb_c, kb, vb, lam, sub_g):
    b, _, t, _ = qb.shape
    c = kb_c.shape[2]
    tq = min(DIFF_TQ, t)
    tk = min(ATT_TK, t)
    return pl.pallas_call(
        functools.partial(_diff_kernel, tk=tk),
        grid=(b, DIFF_HEADS, t // tq),
        in_specs=[pl.BlockSpec((1, 2, tq, HEAD_DIM), lambda bi, h, i: (bi, h, i, 0)),
                  pl.BlockSpec((1, 2, c, HEAD_DIM), lambda bi, h, i: (bi, h, 0, 0)),
                  pl.BlockSpec((1, 1, c, DIFF_V_DIM), lambda bi, h, i: (bi, h, 0, 0)),
                  pl.BlockSpec((1, 2, t, HEAD_DIM), lambda bi, h, i: (bi, h, 0, 0)),
                  pl.BlockSpec((1, 1, t, DIFF_V_DIM), lambda bi, h, i: (bi, h, 0, 0)),
                  pl.BlockSpec((4, HEAD_DIM), lambda bi, h, i: (0, 0)),
                  pl.BlockSpec((1, DIFF_V_DIM), lambda bi, h, i: (0, 0))],
        out_specs=pl.BlockSpec((1, tq, DIFF_V_DIM), lambda bi, h, i: (bi, i, h)),
        out_shape=jax.ShapeDtypeStruct((b, t, DIFF_HEADS * DIFF_V_DIM), BF16),
        compiler_params=_cparams(3, 48),
    )(qb, kb_c, vb_c, kb, vb, lam, sub_g)


def _oproj_kernel(a_ref, d_ref, x_ref, gt_ref, sh_ref, sc_ref, g_ref, wo_ref, wr_ref, br_ref, tri_ref,
                  x1_ref, h2_ref, idx_ref, rank_ref, gate_ref, cnt_ref):
    i = pl.program_id(0)

    @pl.when(i == 0)
    def _():
        cnt_ref[...] = jnp.zeros_like(cnt_ref)

    half = a_ref.shape[1]
    mix = (jnp.dot(a_ref[...], wo_ref[0:half, :], preferred_element_type=F32)
           + jnp.dot(d_ref[...], wo_ref[half:2 * half, :], preferred_element_type=F32))
    x1 = x_ref[...] + gt_ref[0] * mix
    x1_ref[...] = x1
    ms = jnp.mean(x1 * x1, axis=-1, keepdims=True)
    h2 = x1 * lax.rsqrt(ms + EPS) * g_ref[...]
    h2 = h2 * (1.0 + sc_ref[0]) + sh_ref[0]
    h2_ref[...] = h2

    logits = jnp.dot(h2, wr_ref[...], preferred_element_type=F32,
                     precision=lax.Precision.HIGHEST) + br_ref[...]
    lt = logits.T[0:N_EXPERTS, :]
    tm = lt.shape[1]
    eid = lax.broadcasted_iota(jnp.int32, lt.shape, 0).astype(F32)
    vals, sels = [], []
    for k in range(TOP_K):
        m = jnp.max(lt, axis=0, keepdims=True)
        first = jnp.min(jnp.where(lt == m, eid, float(N_EXPERTS)), axis=0, keepdims=True)
        sel = eid == first
        lt = jnp.where(sel, -jnp.inf, lt)
        vals.append(m)
        sels.append(sel)
        idx_ref[k:k + 1, :] = first.astype(jnp.int32)

    onehot = sum(s.astype(F32) for s in sels)
    before = cnt_ref[:, 0:1] + jnp.dot(onehot.astype(BF16), tri_ref[...], preferred_element_type=F32)
    for k in range(TOP_K):
        rank_ref[k:k + 1, :] = jnp.sum(jnp.where(sels[k], before, 0.0), axis=0, keepdims=True).astype(jnp.int32)
    cnt_ref[...] = cnt_ref[...] + jnp.sum(onehot, axis=1, keepdims=True)

    es = [jnp.exp(v - vals[0]) for v in vals]
    den = es[0] + es[1] + es[2] + es[3]
    gates = jnp.concatenate([e / den for e in es] + [jnp.zeros((LANES - TOP_K, tm), F32)], axis=0)
    gate_ref[...] = gates.T


def _oproj_call(a, d, x, gt1, sh2, sc2, g_ffn, w_o, wr_pad, br_pad, tiles_per_batch):
    n, dm = x.shape
    tm = OPROJ_TM
    tri = (lax.broadcasted_iota(jnp.int32, (tm, tm), 0) < lax.broadcasted_iota(jnp.int32, (tm, tm), 1)).astype(BF16)
    row = lambda w: pl.BlockSpec((tm, w), lambda i: (i, 0))
    per_batch = pl.BlockSpec((1, 1, dm), lambda i: (i // tiles_per_batch, 0, 0))
    const = lambda shape: pl.BlockSpec(shape, lambda i: (0,) * len(shape))
    return pl.pallas_call(
        _oproj_kernel,
        grid=(n // tm,),
        in_specs=[row(a.shape[1]), row(d.shape[1]), row(dm), per_batch, per_batch, per_batch,
                  const((1, dm)), const(w_o.shape), const(wr_pad.shape), const((1, LANES)), const((tm, tm))],
        out_specs=[row(dm), row(dm),
                   pl.BlockSpec((TOP_K, tm), lambda i: (0, i)),
                   pl.BlockSpec((TOP_K, tm), lambda i: (0, i)),
                   row(LANES),
                   const((N_EXPERTS, LANES))],
        out_shape=[jax.ShapeDtypeStruct((n, dm), F32),
                   jax.ShapeDtypeStruct((n, dm), F32),
                   jax.ShapeDtypeStruct((TOP_K, n), jnp.int32),
                   jax.ShapeDtypeStruct((TOP_K, n), jnp.int32),
                   jax.ShapeDtypeStruct((n, LANES), F32),
                   jax.ShapeDtypeStruct((N_EXPERTS, LANES), F32)],
        compiler_params=_cparams(1, 48),
    )(a, d, x, gt1, sh2, sc2, g_ffn, w_o, wr_pad, br_pad, tri)


def _row_copy(src_hbm, src_row, dst_ref, dst_row, sem):
    return pltpu.make_async_copy(src_hbm.at[pl.ds(src_row, 1)], dst_ref.at[pl.ds(dst_row, 1)], sem)


def _dispatch_kernel(cnt_ref, pad_ref, start_ref, nused_ref, dest_hbm, h2_ref, xs_hbm,
                     idx_smem, zeros, sem_i, sem_d, sem_z, *, tm, blk):
    i = pl.program_id(0)

    @pl.when(i == 0)
    def _():
        zeros[...] = jnp.zeros_like(zeros)
        for e in range(N_EXPERTS):
            lo = start_ref[e] + cnt_ref[e]
            hi = start_ref[e] + pad_ref[e]
            lax.fori_loop(lo, hi, lambda r, c: (_row_copy(zeros, 0, xs_hbm, r, sem_z).start(), c)[1], 0)
        for e in range(N_EXPERTS):
            lo = start_ref[e] + cnt_ref[e]
            hi = start_ref[e] + pad_ref[e]
            lax.fori_loop(lo, hi, lambda r, c: (_row_copy(zeros, 0, xs_hbm, r, sem_z).wait(), c)[1], 0)

        def tail_copy(j):
            return pltpu.make_async_copy(zeros, xs_hbm.at[pl.ds(pl.multiple_of(j * blk, blk), blk)], sem_z)

        n_blocks = xs_hbm.shape[0] // blk
        lax.fori_loop(nused_ref[0], n_blocks, lambda j, c: (tail_copy(j).start(), c)[1], 0)
        lax.fori_loop(nused_ref[0], n_blocks, lambda j, c: (tail_copy(j).wait(), c)[1], 0)

    idx_copy = pltpu.make_async_copy(dest_hbm.at[pl.ds(i * (TOP_K * tm), TOP_K * tm)], idx_smem, sem_i)
    idx_copy.start()
    idx_copy.wait()

    def issue(t, c):
        for k in range(TOP_K):
            _row_copy(h2_ref, t, xs_hbm, idx_smem[k * tm + t], sem_d).start()
        return c

    def drain(t, c):
        for k in range(TOP_K):
            _row_copy(h2_ref, 0, xs_hbm, 0, sem_d).wait()
        return c

    lax.fori_loop(0, tm, issue, 0)
    lax.fori_loop(0, tm, drain, 0)


def _dispatch_call(counts, padded, starts, n_used, dest_flat, h2, n_rows):
    n, dm = h2.shape
    tm = ROW_TM
    any_spec = pl.BlockSpec(memory_space=pl.ANY)
    return pl.pallas_call(
        functools.partial(_dispatch_kernel, tm=tm, blk=EXPERT_BLK),
        grid_spec=pltpu.PrefetchScalarGridSpec(
            num_scalar_prefetch=4,
            grid=(n // tm,),
            in_specs=[any_spec, pl.BlockSpec((tm, dm), lambda i, *_: (i, 0))],
            out_specs=any_spec,
            scratch_shapes=[pltpu.SMEM((TOP_K * tm,), jnp.int32),
                            pltpu.VMEM((EXPERT_BLK, dm), h2.dtype),
                            pltpu.SemaphoreType.DMA(()), pltpu.SemaphoreType.DMA(()),
                            pltpu.SemaphoreType.DMA(())]),
        out_shape=jax.ShapeDtypeStruct((n_rows, dm), h2.dtype),
        compiler_params=_cparams(1, 16),
    )(counts, padded, starts, n_used, dest_flat, h2)


def _experts_kernel(bexp_ref, nused_ref, xs_ref, win_ref, bin_ref, wout_ref, bout_ref, ys_ref, win_bf, wout_bf):
    i = pl.program_id(0)
    prev = bexp_ref[jnp.maximum(i - 1, 0)]
    new_expert = jnp.logical_or(i == 0, bexp_ref[i] != prev)

    @pl.when(jnp.logical_and(i < nused_ref[0], new_expert))
    def _():
        win_bf[...] = win_ref[0].astype(BF16)
        wout_bf[...] = wout_ref[0].astype(BF16)

    @pl.when(i < nused_ref[0])
    def _():
        dff = wout_bf.shape[0]
        hcat = jnp.dot(xs_ref[...].astype(BF16), win_bf[...], preferred_element_type=F32) + bin_ref[0]
        x_glu = jnp.minimum(hcat[:, :dff], SWIGLU_LIMIT)
        x_lin = jnp.clip(hcat[:, dff:], -SWIGLU_LIMIT, SWIGLU_LIMIT)
        act = x_glu * (1.0 / (1.0 + jnp.exp(-SWIGLU_ALPHA * x_glu))) * (x_lin + 1.0)
        ys_ref[...] = jnp.dot(act.astype(BF16), wout_bf[...], preferred_element_type=F32) + bout_ref[0]

    @pl.when(i >= nused_ref[0])
    def _():
        ys_ref[...] = jnp.zeros_like(ys_ref)


def _experts_call(block_exp, n_used, xs, w_in, b_in, w_out, b_out):
    n_rows, dm = xs.shape
    ne, _, dff2 = w_in.shape
    dff = dff2 // 2
    blk = EXPERT_BLK
    rows = pl.BlockSpec((blk, dm), lambda i, be, nu: (jnp.minimum(i, nu[0] - 1), 0))
    return pl.pallas_call(
        _experts_kernel,
        grid_spec=pltpu.PrefetchScalarGridSpec(
            num_scalar_prefetch=2,
            grid=(n_rows // blk,),
            in_specs=[rows,
                      pl.BlockSpec((1, dm, dff2), lambda i, be, nu: (be[i], 0, 0)),
                      pl.BlockSpec((1, 1, dff2), lambda i, be, nu: (be[i], 0, 0)),
                      pl.BlockSpec((1, dff, dm), lambda i, be, nu: (be[i], 0, 0)),
                      pl.BlockSpec((1, 1, dm), lambda i, be, nu: (be[i], 0, 0))],
            out_specs=pl.BlockSpec((blk, dm), lambda i, be, nu: (i, 0)),
            scratch_shapes=[pltpu.VMEM((dm, dff2), BF16), pltpu.VMEM((dff, dm), BF16)]),
        out_shape=jax.ShapeDtypeStruct((n_rows, dm), F32),
        compiler_params=_cparams(1, 56),
    )(block_exp, n_used, xs, w_in, b_in.reshape(ne, 1, dff2), w_out, b_out.reshape(ne, 1, dm))


def _combine_kernel(dest_hbm, ys_hbm, x1_ref, gate_ref, gt_ref, gf_ref, o_ref, idx_smem, rows, sem_i, sem_d, *, tm):
    i = pl.program_id(0)
    idx_copy = pltpu.make_async_copy(dest_hbm.at[pl.ds(i * (TOP_K * tm), TOP_K * tm)], idx_smem, sem_i)
    idx_copy.start()
    idx_copy.wait()

    def issue(t, c):
        for k in range(TOP_K):
            _row_copy(ys_hbm, idx_smem[k * tm + t], rows.at[k], t, sem_d).start()
        return c

    def drain(t, c):
        for k in range(TOP_K):
            _row_copy(ys_hbm, 0, rows.at[k], 0, sem_d).wait()
        return c

    lax.fori_loop(0, tm, issue, 0)
    lax.fori_loop(0, tm, drain, 0)

    gates = gate_ref[...]
    y = gates[:, 0:1] * rows[0]
    for k in range(1, TOP_K):
        y = y + gates[:, k:k + 1] * rows[k]
    x2 = x1_ref[...] + gt_ref[0] * y
    ms = jnp.mean(x2 * x2, axis=-1, keepdims=True)
    o_ref[...] = x2 * lax.rsqrt(ms + EPS) * gf_ref[...]


def _combine_call(dest_flat, ys, x1, gate_t, gt2, g_final, tiles_per_batch):
    n, dm = x1.shape
    tm = ROW_TM
    row = lambda w: pl.BlockSpec((tm, w), lambda i: (i, 0))
    return pl.pallas_call(
        functools.partial(_combine_kernel, tm=tm),
        grid=(n // tm,),
        in_specs=[pl.BlockSpec(memory_space=pl.ANY), pl.BlockSpec(memory_space=pl.ANY),
                  row(dm), row(LANES),
                  pl.BlockSpec((1, 1, dm), lambda i: (i // tiles_per_batch, 0, 0)),
                  pl.BlockSpec((1, dm), lambda i: (0, 0))],
        out_specs=row(dm),
        out_shape=jax.ShapeDtypeStruct((n, dm), F32),
        scratch_shapes=[pltpu.SMEM((TOP_K * tm,), jnp.int32),
                        pltpu.VMEM((TOP_K, tm, dm), F32),
                        pltpu.SemaphoreType.DMA(()), pltpu.SemaphoreType.DMA(())],
        compiler_params=_cparams(1, 32),
    )(dest_flat, ys, x1, gate_t, gt2, g_final)


def _rope_tables(t):
    rows = t // GRID_W
    n_freq = HEAD_DIM // 4
    inv_freq = ROPE_THETA ** (-jnp.arange(n_freq, dtype=F32) / n_freq)
    ar = jnp.arange(rows, dtype=F32)[:, None] * inv_freq
    ac = jnp.arange(GRID_W, dtype=F32)[:, None] * inv_freq
    per_row = lambda a: jnp.broadcast_to(a[:, None, :], (rows, GRID_W, n_freq))
    per_col = lambda a: jnp.broadcast_to(a[None, :, :], (rows, GRID_W, n_freq))
    cr, sr, cc, sc = per_row(jnp.cos(ar)), per_row(jnp.sin(ar)), per_col(jnp.cos(ac)), per_col(jnp.sin(ac))
    cos = jnp.concatenate([cr, cr, cc, cc] * 2, axis=-1).reshape(t, LANES)
    sin = jnp.concatenate([-sr, sr, -sc, sc] * 2, axis=-1).reshape(t, LANES)
    return cos, sin


def kernel(x, c, ctx, c_ctx, w_ada, b_ada, g_attn, w_qkv, gqa_q_norm, gqa_k_norm, diff_lambda,
           diff_subln, w_o, g_ffn, w_router, b_router, w_in, b_in, w_out, b_out, g_final):
    assert w_ada.shape[0] == 1, "single-layer block"
    b, t, dm = x.shape
    n = b * t

    cv = jnp.zeros((8, dm), F32).at[:b].set(c).at[b].set(c_ctx)
    mod = _mod_call(cv, w_ada[0], b_ada[0])
    sh1, sc1, gt1, sh2, sc2, gt2 = [m[:b, None, :] for m in jnp.split(mod, N_MOD, axis=-1)]
    csh1, csc1 = [jnp.broadcast_to(m[b][None, None, :], (b, 1, dm)) for m in jnp.split(mod, N_MOD, axis=-1)[:2]]

    wq = w_qkv[0]
    o_ka, o_va, o_qb, o_kb, o_vb = 512, 640, 768, 1280, 1792
    w_kv = jnp.concatenate([wq[:, o_ka:o_qb], wq[:, o_kb:]], axis=1)
    w_all = jnp.concatenate([w_kv, wq[:, :o_ka], wq[:, o_qb:o_kb]], axis=1).astype(BF16)
    gq = jnp.tile(gqa_q_norm[0], 2)[None, :]
    gk = jnp.tile(gqa_k_norm[0], 2)[None, :]
    lane = jnp.arange(LANES)
    bd = (lane[:, None] // HEAD_DIM == lane[None, :] // HEAD_DIM).astype(BF16)
    cos, sin = _rope_tables(t)
    g_a = g_attn[0][None, :]

    ka, va, kb, vb, qa, qb = _proj_call(x, sh1, sc1, g_a, w_all, cos, sin, gq, gk, bd, rope=True, with_q=True)
    ka_c, va_c, kb_c, vb_c = _proj_call(ctx, csh1, csc1, g_a, w_all[:, :KV_COLS], cos, sin, gq, gk, bd,
                                        rope=False, with_q=False)

    att_a = _gqa_call(qa, ka_c, va_c, ka, va)
    att_d = _diff_call(qb, kb_c, vb_c, kb, vb, diff_lambda[0], diff_subln[0][None, :])

    wr_pad = jnp.zeros((dm, LANES), F32).at[:, :N_EXPERTS].set(w_router[0])
    br_pad = jnp.zeros((1, LANES), F32).at[0, :N_EXPERTS].set(b_router[0])
    x1, h2, idx, rank, gate_t, cnt = _oproj_call(
        att_a.reshape(n, -1), att_d.reshape(n, -1), x.reshape(n, dm), gt1, sh2, sc2, g_ffn[0][None, :],
        w_o[0].astype(BF16), wr_pad, br_pad, t // OPROJ_TM)

    blk = EXPERT_BLK
    counts = cnt[:, 0].astype(jnp.int32)
    padded = (counts + blk - 1) // blk * blk
    pad_ends = jnp.cumsum(padded)
    starts = pad_ends - padded
    n_blocks = n * TOP_K // blk + N_EXPERTS
    expert_ids = jnp.arange(N_EXPERTS, dtype=jnp.int32)
    dest = rank + jnp.sum(jnp.where(idx[..., None] == expert_ids, starts, 0), axis=-1)
    dest_flat = dest.reshape(TOP_K, n // ROW_TM, ROW_TM).transpose(1, 0, 2).reshape(-1)
    block_row0 = jnp.arange(n_blocks, dtype=jnp.int32) * blk
    block_exp = jnp.minimum(jnp.sum((pad_ends[None, :] <= block_row0[:, None]).astype(jnp.int32), axis=1),
                            N_EXPERTS - 1)
    n_used = (pad_ends[-1:] // blk).astype(jnp.int32)

    xs = _dispatch_call(counts, padded, starts, n_used, dest_flat, h2, n_blocks * blk)
    ys = _experts_call(block_exp, n_used, xs, w_in[0], b_in[0], w_out[0], b_out[0])
    out = _combine_call(dest_flat, ys, x1, gate_t, gt2, g_final[None, :], t // ROW_TM)
    return out.reshape(b, t, dm)
```

```python
import functools

import jax
import jax.numpy as jnp
from jax import lax
from jax.experimental import pallas as pl
from jax.experimental.pallas import tpu as pltpu

F32 = jnp.float32
BF16 = jnp.bfloat16

HEAD_DIM = 64
GQA_HEADS = 8
GQA_KV_HEADS = 2
GQA_GROUP = GQA_HEADS // GQA_KV_HEADS
DIFF_HEADS = 4
DIFF_V_DIM = 2 * HEAD_DIM
GRID_W = 64
ROPE_THETA = 10000.0
N_EXPERTS = 32
TOP_K = 4
SWIGLU_ALPHA = 1.702
SWIGLU_LIMIT = 7.0
N_MOD = 6
EPS = 1e-6
LAMBDA_INIT = 0.8 - 0.6
LOG2E = 1.4426950408889634
Q_SCALE = HEAD_DIM ** -0.5 * LOG2E

LANES = 128
KV_COLS = 2 * GQA_KV_HEADS * HEAD_DIM + DIFF_HEADS * 2 * HEAD_DIM + DIFF_HEADS * DIFF_V_DIM
Q_COLS = GQA_HEADS * HEAD_DIM + DIFF_HEADS * 2 * HEAD_DIM

PROJ_TM = 256
GQA_TQ = 128
DIFF_TQ = 256
ATT_TK = 2816
MXU_TILE = 256
OPROJ_TM = 256
DISPATCH_TM = 512
COMBINE_TM = 256
EXPERT_BLK = 512


def _cparams(n_axes, vmem_mb):
    return pltpu.CompilerParams(dimension_semantics=("arbitrary",) * n_axes,
                                vmem_limit_bytes=vmem_mb * 1024 * 1024)


def _mod_kernel(cv_ref, w_ref, b_ref, o_ref):
    cv = cv_ref[...]
    s = cv * (1.0 / (1.0 + jnp.exp(-cv)))
    o_ref[...] = jnp.dot(s, w_ref[...], preferred_element_type=F32,
                         precision=lax.Precision.HIGHEST) + b_ref[...]


def _mod_call(cv, w_ada, b_ada):
    d, n = w_ada.shape
    tn = 1024
    return pl.pallas_call(
        _mod_kernel,
        grid=(n // tn,),
        in_specs=[pl.BlockSpec((8, d), lambda j: (0, 0)),
                  pl.BlockSpec((d, tn), lambda j: (0, j)),
                  pl.BlockSpec((1, tn), lambda j: (0, j))],
        out_specs=pl.BlockSpec((8, tn), lambda j: (0, j)),
        out_shape=jax.ShapeDtypeStruct((8, n), F32),
        compiler_params=_cparams(1, 32),
    )(cv, w_ada, b_ada.reshape(1, n))


def _head_rms(v, gain, bd):
    sq = v * v
    hi = sq.astype(BF16)
    lo = (sq - hi.astype(F32)).astype(BF16)
    ss = (jnp.dot(hi, bd, preferred_element_type=F32) + jnp.dot(lo, bd, preferred_element_type=F32))
    return v * lax.rsqrt(ss * (1.0 / HEAD_DIM) + EPS) * gain


def _rope_chunk(v, cos, sin, first_half):
    partner = jnp.where(first_half, pltpu.roll(v, LANES - 16, 1), pltpu.roll(v, 16, 1))
    return v * cos + partner * sin


def _proj_kernel(x_ref, sh_ref, sc_ref, g_ref, w_ref, cos_ref, sin_ref, gq_ref, gk_ref, bd_ref,
                 *out_refs, rope, with_q):
    xf = x_ref[0]
    ms = jnp.mean(xf * xf, axis=-1, keepdims=True)
    h = xf * lax.rsqrt(ms + EPS) * g_ref[...]
    h = h * (1.0 + sc_ref[0]) + sh_ref[0]
    p = jnp.dot(h.astype(BF16), w_ref[...], preferred_element_type=F32)

    if with_q:
        ka_ref, va_ref, kb_ref, vb_ref, qa_ref, qb_ref = out_refs
    else:
        ka_ref, va_ref, kb_ref, vb_ref = out_refs
    bd = bd_ref[...]
    if rope:
        cos = cos_ref[...]
        sin = sin_ref[...]
        lane = lax.broadcasted_iota(jnp.int32, cos.shape, 1)
        first_half = (lane % 32) < 16

    def chunk(j):
        return p[:, j * LANES:(j + 1) * LANES]

    def maybe_rope(v):
        return _rope_chunk(v, cos, sin, first_half) if rope else v

    def put_heads(ref, j, v):
        ref[0, 2 * j] = v[:, :HEAD_DIM].astype(ref.dtype)
        ref[0, 2 * j + 1] = v[:, HEAD_DIM:].astype(ref.dtype)

    put_heads(ka_ref, 0, maybe_rope(_head_rms(chunk(0), gk_ref[...], bd)))
    put_heads(va_ref, 0, chunk(1))
    for j in range(4):
        put_heads(kb_ref, j, maybe_rope(chunk(2 + j)))
        vb_ref[0, j] = chunk(6 + j).astype(vb_ref.dtype)
    if with_q:
        for j in range(4):
            put_heads(qa_ref, j, maybe_rope(_head_rms(chunk(10 + j), gq_ref[...], bd)) * Q_SCALE)
            put_heads(qb_ref, j, maybe_rope(chunk(14 + j)) * Q_SCALE)


def _proj_call(x, shift, scale, g, w, cos, sin, gq, gk, bd, *, rope, with_q):
    nb, t, d = x.shape
    tm = min(PROJ_TM, t)
    ncols = w.shape[1]
    head = lambda n: pl.BlockSpec((1, n, tm, HEAD_DIM), lambda b, i: (b, 0, i, 0))
    out_specs = [head(2), head(2), head(8), pl.BlockSpec((1, 4, tm, DIFF_V_DIM), lambda b, i: (b, 0, i, 0))]
    out_shape = [jax.ShapeDtypeStruct((nb, 2, t, HEAD_DIM), BF16),
                 jax.ShapeDtypeStruct((nb, 2, t, HEAD_DIM), BF16),
                 jax.ShapeDtypeStruct((nb, 8, t, HEAD_DIM), BF16),
                 jax.ShapeDtypeStruct((nb, 4, t, DIFF_V_DIM), BF16)]
    if with_q:
        out_specs += [head(8), head(8)]
        out_shape += [jax.ShapeDtypeStruct((nb, 8, t, HEAD_DIM), BF16)] * 2
    const = lambda shape: pl.BlockSpec(shape, lambda b, i: (0,) * len(shape))
    return pl.pallas_call(
        functools.partial(_proj_kernel, rope=rope, with_q=with_q),
        grid=(nb, t // tm),
        in_specs=[pl.BlockSpec((1, tm, d), lambda b, i: (b, i, 0)),
                  pl.BlockSpec((1, 1, d), lambda b, i: (b, 0, 0)),
                  pl.BlockSpec((1, 1, d), lambda b, i: (b, 0, 0)),
                  const((1, d)),
                  const((d, ncols)),
                  pl.BlockSpec((tm, LANES), lambda b, i: (i, 0)),
                  pl.BlockSpec((tm, LANES), lambda b, i: (i, 0)),
                  const((1, LANES)), const((1, LANES)), const((LANES, LANES))],
        out_specs=out_specs,
        out_shape=out_shape,
        compiler_params=_cparams(2, 48),
    )(x, shift, scale, g, w, cos, sin, gq, gk, bd)


def _scores(q, k):
    return lax.dot_general(q, k, (((1,), (1,)), ((), ())), preferred_element_type=F32)


def _softmax_step(s, v, carry):
    m, l, acc = carry
    m_new = jnp.maximum(m, jnp.max(s, axis=1, keepdims=True))
    alpha = jnp.exp2(m - m_new)
    p = jnp.exp2(s - m_new)
    l_new = alpha * l + jnp.sum(p, axis=1, keepdims=True)
    acc_new = alpha * acc + jnp.dot(p.astype(BF16), v, preferred_element_type=F32)
    return m_new, l_new, acc_new


def _init_carry(m_rows, dv):
    return (jnp.full((m_rows, 1), -jnp.inf, F32), jnp.zeros((m_rows, 1), F32), jnp.zeros((m_rows, dv), F32))


def _key_tile(n_keys):
    return max(tk for tk in range(MXU_TILE, min(ATT_TK, n_keys) + 1, MXU_TILE) if n_keys % tk == 0)


def _attend(scores, k_count, v_ref, m_rows, dv, tk):
    carry = _init_carry(m_rows, dv)
    for j in range(k_count // tk):
        rows = pl.ds(j * tk, tk)
        carry = _softmax_step(scores(rows), v_ref[0, 0, rows, :], carry)
    _, l, acc = carry
    return acc / l


def _gqa_kernel(q_ref, k_ref, v_ref, o_ref, *, tk):
    tq = q_ref.shape[2]
    q = q_ref[0].reshape(GQA_GROUP * tq, HEAD_DIM)
    o = _attend(lambda rows: _scores(q, k_ref[0, 0, rows, :]), k_ref.shape[2], v_ref,
                GQA_GROUP * tq, HEAD_DIM, tk).reshape(GQA_GROUP, tq, HEAD_DIM)
    for g in range(GQA_GROUP):
        o_ref[0, :, g * HEAD_DIM:(g + 1) * HEAD_DIM] = o[g].astype(o_ref.dtype)


def _gqa_call(qa, ka, va):
    b, _, t, _ = qa.shape
    nk = ka.shape[2]
    tq = min(GQA_TQ, t)
    kv = pl.BlockSpec((1, 1, nk, HEAD_DIM), lambda bi, h, i: (bi, h, 0, 0))
    return pl.pallas_call(
        functools.partial(_gqa_kernel, tk=_key_tile(nk)),
        grid=(b, GQA_KV_HEADS, t // tq),
        in_specs=[pl.BlockSpec((1, GQA_GROUP, tq, HEAD_DIM), lambda bi, h, i: (bi, h, i, 0)), kv, kv],
        out_specs=pl.BlockSpec((1, tq, GQA_GROUP * HEAD_DIM), lambda bi, h, i: (bi, i, h)),
        out_shape=jax.ShapeDtypeStruct((b, t, GQA_HEADS * HEAD_DIM), BF16),
        compiler_params=_cparams(3, 48),
    )(qa, ka, va)


def _diff_kernel(q_ref, k_ref, v_ref, lam_ref, sg_ref, o_ref, *, tk):
    tq = q_ref.shape[2]
    q = (q_ref[0, 0], q_ref[0, 1])

    def scores(rows):
        return jnp.concatenate([_scores(q[c], k_ref[0, c, rows, :]) for c in range(2)], axis=0)

    o_both = _attend(scores, k_ref.shape[2], v_ref, 2 * tq, DIFF_V_DIM, tk)
    lamf = lam_ref[...]
    lam = (jnp.exp(jnp.sum(lamf[0:1] * lamf[1:2], axis=-1, keepdims=True))
           - jnp.exp(jnp.sum(lamf[2:3] * lamf[3:4], axis=-1, keepdims=True)) + LAMBDA_INIT)
    o = o_both[:tq] - lam * o_both[tq:]
    ms = jnp.mean(o * o, axis=-1, keepdims=True)
    o = o * lax.rsqrt(ms + EPS) * sg_ref[...] * (1.0 - LAMBDA_INIT)
    o_ref[0] = o.astype(o_ref.dtype)


def _diff_call(qb, kb, vb, lam, sub_g):
    b, _, t, _ = qb.shape
    nk = kb.shape[2]
    tq = min(DIFF_TQ, t)
    return pl.pallas_call(
        functools.partial(_diff_kernel, tk=_key_tile(nk)),
        grid=(b, DIFF_HEADS, t // tq),
        in_specs=[pl.BlockSpec((1, 2, tq, HEAD_DIM), lambda bi, h, i: (bi, h, i, 0)),
                  pl.BlockSpec((1, 2, nk, HEAD_DIM), lambda bi, h, i: (bi, h, 0, 0)),
                  pl.BlockSpec((1, 1, nk, DIFF_V_DIM), lambda bi, h, i: (bi, h, 0, 0)),
                  pl.BlockSpec((4, HEAD_DIM), lambda bi, h, i: (0, 0)),
                  pl.BlockSpec((1, DIFF_V_DIM), lambda bi, h, i: (0, 0))],
        out_specs=pl.BlockSpec((1, tq, DIFF_V_DIM), lambda bi, h, i: (bi, i, h)),
        out_shape=jax.ShapeDtypeStruct((b, t, DIFF_HEADS * DIFF_V_DIM), BF16),
        compiler_params=_cparams(3, 48),
    )(qb, kb, vb, lam, sub_g)


def _oproj_kernel(a_ref, d_ref, x_ref, gt_ref, sh_ref, sc_ref, g_ref, wo_ref, wr_ref, br_ref, tri_ref,
                  x1_ref, h2_ref, idx_ref, rank_ref, gate_ref, cnt_ref):
    i = pl.program_id(0)

    @pl.when(i == 0)
    def _():
        cnt_ref[...] = jnp.zeros_like(cnt_ref)

    half = a_ref.shape[1]
    mix = (jnp.dot(a_ref[...], wo_ref[0:half, :], preferred_element_type=F32)
           + jnp.dot(d_ref[...], wo_ref[half:2 * half, :], preferred_element_type=F32))
    x1 = x_ref[...] + gt_ref[0] * mix
    x1_ref[...] = x1
    ms = jnp.mean(x1 * x1, axis=-1, keepdims=True)
    h2 = x1 * lax.rsqrt(ms + EPS) * g_ref[...]
    h2 = h2 * (1.0 + sc_ref[0]) + sh_ref[0]
    h2_ref[...] = h2

    logits = jnp.dot(h2, wr_ref[...], preferred_element_type=F32,
                     precision=lax.Precision.HIGHEST) + br_ref[...]
    lt = logits.T[0:N_EXPERTS, :]
    tm = lt.shape[1]
    eid = lax.broadcasted_iota(jnp.int32, lt.shape, 0).astype(F32)
    vals, sels = [], []
    for k in range(TOP_K):
        m = jnp.max(lt, axis=0, keepdims=True)
        first = jnp.min(jnp.where(lt == m, eid, float(N_EXPERTS)), axis=0, keepdims=True)
        sel = eid == first
        lt = jnp.where(sel, -jnp.inf, lt)
        vals.append(m)
        sels.append(sel)
        idx_ref[k:k + 1, :] = first.astype(jnp.int32)

    onehot = sum(s.astype(F32) for s in sels)
    before = cnt_ref[:, 0:1] + jnp.dot(onehot.astype(BF16), tri_ref[...], preferred_element_type=F32)
    for k in range(TOP_K):
        rank_ref[k:k + 1, :] = jnp.sum(jnp.where(sels[k], before, 0.0), axis=0, keepdims=True).astype(jnp.int32)
    cnt_ref[...] = cnt_ref[...] + jnp.sum(onehot, axis=1, keepdims=True)

    es = [jnp.exp(v - vals[0]) for v in vals]
    den = es[0] + es[1] + es[2] + es[3]
    gates = jnp.concatenate([e / den for e in es] + [jnp.zeros((LANES - TOP_K, tm), F32)], axis=0)
    gate_ref[...] = gates.T


def _oproj_call(a, d, x, gt1, sh2, sc2, g_ffn, w_o, wr_pad, br_pad, tiles_per_batch):
    n, dm = x.shape
    tm = OPROJ_TM
    tri = (lax.broadcasted_iota(jnp.int32, (tm, tm), 0) < lax.broadcasted_iota(jnp.int32, (tm, tm), 1)).astype(BF16)
    row = lambda w: pl.BlockSpec((tm, w), lambda i: (i, 0))
    per_batch = pl.BlockSpec((1, 1, dm), lambda i: (i // tiles_per_batch, 0, 0))
    const = lambda shape: pl.BlockSpec(shape, lambda i: (0,) * len(shape))
    return pl.pallas_call(
        _oproj_kernel,
        grid=(n // tm,),
        in_specs=[row(a.shape[1]), row(d.shape[1]), row(dm), per_batch, per_batch, per_batch,
                  const((1, dm)), const(w_o.shape), const(wr_pad.shape), const((1, LANES)), const((tm, tm))],
        out_specs=[row(dm), row(dm),
                   pl.BlockSpec((TOP_K, tm), lambda i: (0, i)),
                   pl.BlockSpec((TOP_K, tm), lambda i: (0, i)),
                   row(LANES),
                   const((N_EXPERTS, LANES))],
        out_shape=[jax.ShapeDtypeStruct((n, dm), F32),
                   jax.ShapeDtypeStruct((n, dm), F32),
                   jax.ShapeDtypeStruct((TOP_K, n), jnp.int32),
                   jax.ShapeDtypeStruct((TOP_K, n), jnp.int32),
                   jax.ShapeDtypeStruct((n, LANES), F32),
                   jax.ShapeDtypeStruct((N_EXPERTS, LANES), F32)],
        compiler_params=_cparams(1, 48),
    )(a, d, x, gt1, sh2, sc2, g_ffn, w_o, wr_pad, br_pad, tri)


def _row_copy(src_hbm, src_row, dst_ref, dst_row, sem):
    return pltpu.make_async_copy(src_hbm.at[pl.ds(src_row, 1)], dst_ref.at[pl.ds(dst_row, 1)], sem)


def _dispatch_kernel(cnt_ref, pad_ref, start_ref, nused_ref, dest_hbm, h2_ref, xs_hbm,
                     idx_smem, zeros, sem_i, sem_d, sem_z, *, tm, blk):
    i = pl.program_id(0)

    @pl.when(i == 0)
    def _():
        zeros[...] = jnp.zeros_like(zeros)
        for e in range(N_EXPERTS):
            lo = start_ref[e] + cnt_ref[e]
            hi = start_ref[e] + pad_ref[e]
            lax.fori_loop(lo, hi, lambda r, c: (_row_copy(zeros, 0, xs_hbm, r, sem_z).start(), c)[1], 0)
        for e in range(N_EXPERTS):
            lo = start_ref[e] + cnt_ref[e]
            hi = start_ref[e] + pad_ref[e]
            lax.fori_loop(lo, hi, lambda r, c: (_row_copy(zeros, 0, xs_hbm, r, sem_z).wait(), c)[1], 0)

        def tail_copy(j):
            return pltpu.make_async_copy(zeros, xs_hbm.at[pl.ds(pl.multiple_of(j * blk, blk), blk)], sem_z)

        n_blocks = xs_hbm.shape[0] // blk
        lax.fori_loop(nused_ref[0], n_blocks, lambda j, c: (tail_copy(j).start(), c)[1], 0)
        lax.fori_loop(nused_ref[0], n_blocks, lambda j, c: (tail_copy(j).wait(), c)[1], 0)

    idx_copy = pltpu.make_async_copy(dest_hbm.at[pl.ds(i * (TOP_K * tm), TOP_K * tm)], idx_smem, sem_i)
    idx_copy.start()
    idx_copy.wait()

    def issue(t, c):
        for k in range(TOP_K):
            _row_copy(h2_ref, t, xs_hbm, idx_smem[k * tm + t], sem_d).start(priority=k % 2)
        return c

    def drain(t, c):
        for k in range(TOP_K):
            _row_copy(h2_ref, 0, xs_hbm, 0, sem_d).wait()
        return c

    lax.fori_loop(0, tm, issue, 0)
    lax.fori_loop(0, tm, drain, 0)


def _dispatch_call(counts, padded, starts, n_used, dest_flat, h2, n_rows, tm):
    n, dm = h2.shape
    any_spec = pl.BlockSpec(memory_space=pl.ANY)
    return pl.pallas_call(
        functools.partial(_dispatch_kernel, tm=tm, blk=EXPERT_BLK),
        grid_spec=pltpu.PrefetchScalarGridSpec(
            num_scalar_prefetch=4,
            grid=(n // tm,),
            in_specs=[any_spec, pl.BlockSpec((tm, dm), lambda i, *_: (i, 0))],
            out_specs=any_spec,
            scratch_shapes=[pltpu.SMEM((TOP_K * tm,), jnp.int32),
                            pltpu.VMEM((EXPERT_BLK, dm), h2.dtype),
                            pltpu.SemaphoreType.DMA(()), pltpu.SemaphoreType.DMA(()),
                            pltpu.SemaphoreType.DMA(())]),
        out_shape=jax.ShapeDtypeStruct((n_rows, dm), h2.dtype),
        compiler_params=_cparams(1, 24),
    )(counts, padded, starts, n_used, dest_flat, h2)


def _experts_kernel(bexp_ref, nused_ref, xs_ref, win_ref, bin_ref, wout_ref, bout_ref, ys_ref, win_bf, wout_bf):
    i = pl.program_id(0)
    prev = bexp_ref[jnp.maximum(i - 1, 0)]
    new_expert = jnp.logical_or(i == 0, bexp_ref[i] != prev)

    @pl.when(jnp.logical_and(i < nused_ref[0], new_expert))
    def _():
        win_bf[...] = win_ref[0].astype(BF16)
        wout_bf[...] = wout_ref[0].astype(BF16)

    @pl.when(i < nused_ref[0])
    def _():
        dff = wout_bf.shape[0]
        hcat = jnp.dot(xs_ref[...].astype(BF16), win_bf[...], preferred_element_type=F32) + bin_ref[0]
        x_glu = jnp.minimum(hcat[:, :dff], SWIGLU_LIMIT)
        x_lin = jnp.clip(hcat[:, dff:], -SWIGLU_LIMIT, SWIGLU_LIMIT)
        act = x_glu * (1.0 / (1.0 + jnp.exp(-SWIGLU_ALPHA * x_glu))) * (x_lin + 1.0)
        ys_ref[...] = jnp.dot(act.astype(BF16), wout_bf[...], preferred_element_type=F32) + bout_ref[0]

    @pl.when(i >= nused_ref[0])
    def _():
        ys_ref[...] = jnp.zeros_like(ys_ref)


def _experts_call(block_exp, n_used, xs, w_in, b_in, w_out, b_out):
    n_rows, dm = xs.shape
    ne, _, dff2 = w_in.shape
    dff = dff2 // 2
    blk = EXPERT_BLK
    rows = pl.BlockSpec((blk, dm), lambda i, be, nu: (jnp.minimum(i, nu[0] - 1), 0))
    return pl.pallas_call(
        _experts_kernel,
        grid_spec=pltpu.PrefetchScalarGridSpec(
            num_scalar_prefetch=2,
            grid=(n_rows // blk,),
            in_specs=[rows,
                      pl.BlockSpec((1, dm, dff2), lambda i, be, nu: (be[i], 0, 0)),
                      pl.BlockSpec((1, 1, dff2), lambda i, be, nu: (be[i], 0, 0)),
                      pl.BlockSpec((1, dff, dm), lambda i, be, nu: (be[i], 0, 0)),
                      pl.BlockSpec((1, 1, dm), lambda i, be, nu: (be[i], 0, 0))],
            out_specs=pl.BlockSpec((blk, dm), lambda i, be, nu: (i, 0)),
            scratch_shapes=[pltpu.VMEM((dm, dff2), BF16), pltpu.VMEM((dff, dm), BF16)]),
        out_shape=jax.ShapeDtypeStruct((n_rows, dm), F32),
        compiler_params=_cparams(1, 56),
    )(block_exp, n_used, xs, w_in, b_in.reshape(ne, 1, dff2), w_out, b_out.reshape(ne, 1, dm))


def _combine_kernel(dest_hbm, ys_hbm, x1_ref, gate_ref, gt_ref, gf_ref, o_ref, idx_smem, rows, sem_i, sem_d, *, tm):
    i = pl.program_id(0)

    def fetch(step, slot):
        idx_copy = pltpu.make_async_copy(dest_hbm.at[pl.ds(step * (TOP_K * tm), TOP_K * tm)], idx_smem, sem_i)
        idx_copy.start()
        idx_copy.wait()

        def issue(t, c):
            for k in range(TOP_K):
                _row_copy(ys_hbm, idx_smem[k * tm + t], rows.at[slot, k], t, sem_d.at[slot]).start(priority=k % 2)
            return c

        lax.fori_loop(0, tm, issue, 0)

    @pl.when(i == 0)
    def _():
        fetch(0, 0)

    for parity in range(2):
        @pl.when(jnp.logical_and(i + 1 < pl.num_programs(0), (i + 1) % 2 == parity))
        def _():
            fetch(i + 1, parity)

    slot = i % 2

    def drain(t, c):
        for k in range(TOP_K):
            _row_copy(ys_hbm, 0, rows.at[slot, k], 0, sem_d.at[slot]).wait()
        return c

    lax.fori_loop(0, tm, drain, 0)

    gates = gate_ref[...]
    y = gates[:, 0:1] * rows[slot, 0]
    for k in range(1, TOP_K):
        y = y + gates[:, k:k + 1] * rows[slot, k]
    x2 = x1_ref[...] + gt_ref[0] * y
    ms = jnp.mean(x2 * x2, axis=-1, keepdims=True)
    o_ref[...] = x2 * lax.rsqrt(ms + EPS) * gf_ref[...]


def _combine_call(dest_flat, ys, x1, gate_t, gt2, g_final, tiles_per_batch, tm):
    n, dm = x1.shape
    row = lambda w: pl.BlockSpec((tm, w), lambda i: (i, 0))
    return pl.pallas_call(
        functools.partial(_combine_kernel, tm=tm),
        grid=(n // tm,),
        in_specs=[pl.BlockSpec(memory_space=pl.ANY), pl.BlockSpec(memory_space=pl.ANY),
                  row(dm), row(LANES),
                  pl.BlockSpec((1, 1, dm), lambda i: (i // tiles_per_batch, 0, 0)),
                  pl.BlockSpec((1, dm), lambda i: (0, 0))],
        out_specs=row(dm),
        out_shape=jax.ShapeDtypeStruct((n, dm), F32),
        scratch_shapes=[pltpu.SMEM((TOP_K * tm,), jnp.int32),
                        pltpu.VMEM((2, TOP_K, tm, dm), F32),
                        pltpu.SemaphoreType.DMA(()), pltpu.SemaphoreType.DMA((2,))],
        compiler_params=_cparams(1, 40),
    )(dest_flat, ys, x1, gate_t, gt2, g_final)


def _rope_tables(t):
    rows = t // GRID_W
    n_freq = HEAD_DIM // 4
    inv_freq = ROPE_THETA ** (-jnp.arange(n_freq, dtype=F32) / n_freq)
    ar = jnp.arange(rows, dtype=F32)[:, None] * inv_freq
    ac = jnp.arange(GRID_W, dtype=F32)[:, None] * inv_freq
    per_row = lambda a: jnp.broadcast_to(a[:, None, :], (rows, GRID_W, n_freq))
    per_col = lambda a: jnp.broadcast_to(a[None, :, :], (rows, GRID_W, n_freq))
    cr, sr, cc, sc = per_row(jnp.cos(ar)), per_row(jnp.sin(ar)), per_col(jnp.cos(ac)), per_col(jnp.sin(ac))
    cos = jnp.concatenate([cr, cr, cc, cc] * 2, axis=-1).reshape(t, LANES)
    sin = jnp.concatenate([-sr, sr, -sc, sc] * 2, axis=-1).reshape(t, LANES)
    return cos, sin


def kernel(x, c, ctx, c_ctx, w_ada, b_ada, g_attn, w_qkv, gqa_q_norm, gqa_k_norm, diff_lambda,
           diff_subln, w_o, g_ffn, w_router, b_router, w_in, b_in, w_out, b_out, g_final):
    assert w_ada.shape[0] == 1, "single-layer block"
    b, t, dm = x.shape
    n = b * t

    cv = jnp.zeros((8, dm), F32).at[:b].set(c).at[b].set(c_ctx)
    mod = _mod_call(cv, w_ada[0], b_ada[0])
    sh1, sc1, gt1, sh2, sc2, gt2 = [m[:b, None, :] for m in jnp.split(mod, N_MOD, axis=-1)]
    csh1, csc1 = [jnp.broadcast_to(m[b][None, None, :], (b, 1, dm)) for m in jnp.split(mod, N_MOD, axis=-1)[:2]]

    wq = w_qkv[0]
    o_ka, o_va, o_qb, o_kb, o_vb = 512, 640, 768, 1280, 1792
    w_kv = jnp.concatenate([wq[:, o_ka:o_qb], wq[:, o_kb:]], axis=1)
    w_all = jnp.concatenate([w_kv, wq[:, :o_ka], wq[:, o_qb:o_kb]], axis=1).astype(BF16)
    gq = jnp.tile(gqa_q_norm[0], 2)[None, :]
    gk = jnp.tile(gqa_k_norm[0], 2)[None, :]
    lane = jnp.arange(LANES)
    bd = (lane[:, None] // HEAD_DIM == lane[None, :] // HEAD_DIM).astype(BF16)
    cos, sin = _rope_tables(t)
    g_a = g_attn[0][None, :]

    ka, va, kb, vb, qa, qb = _proj_call(x, sh1, sc1, g_a, w_all, cos, sin, gq, gk, bd, rope=True, with_q=True)
    ka_c, va_c, kb_c, vb_c = _proj_call(ctx, csh1, csc1, g_a, w_all[:, :KV_COLS], cos, sin, gq, gk, bd,
                                        rope=False, with_q=False)

    keys = lambda ctx_part, lat_part: jnp.concatenate([ctx_part, lat_part], axis=2)
    att_a = _gqa_call(qa, keys(ka_c, ka), keys(va_c, va))
    att_d = _diff_call(qb, keys(kb_c, kb), keys(vb_c, vb), diff_lambda[0], diff_subln[0][None, :])

    wr_pad = jnp.zeros((dm, LANES), F32).at[:, :N_EXPERTS].set(w_router[0])
    br_pad = jnp.zeros((1, LANES), F32).at[0, :N_EXPERTS].set(b_router[0])
    x1, h2, idx, rank, gate_t, cnt = _oproj_call(
        att_a.reshape(n, -1), att_d.reshape(n, -1), x.reshape(n, dm), gt1, sh2, sc2, g_ffn[0][None, :],
        w_o[0].astype(BF16), wr_pad, br_pad, t // OPROJ_TM)

    blk = EXPERT_BLK
    counts = cnt[:, 0].astype(jnp.int32)
    padded = (counts + blk - 1) // blk * blk
    pad_ends = jnp.cumsum(padded)
    starts = pad_ends - padded
    n_blocks = n * TOP_K // blk + N_EXPERTS
    expert_ids = jnp.arange(N_EXPERTS, dtype=jnp.int32)
    dest = rank + jnp.sum(jnp.where(idx[..., None] == expert_ids, starts, 0), axis=-1)
    tiled = lambda tm: dest.reshape(TOP_K, n // tm, tm).transpose(1, 0, 2).reshape(-1)
    block_row0 = jnp.arange(n_blocks, dtype=jnp.int32) * blk
    block_exp = jnp.minimum(jnp.sum((pad_ends[None, :] <= block_row0[:, None]).astype(jnp.int32), axis=1),
                            N_EXPERTS - 1)
    n_used = (pad_ends[-1:] // blk).astype(jnp.int32)

    dtm = min(DISPATCH_TM, t)
    ctm = min(COMBINE_TM, t)
    xs = _dispatch_call(counts, padded, starts, n_used, tiled(dtm), h2, n_blocks * blk, dtm)
    ys = _experts_call(block_exp, n_used, xs, w_in[0], b_in[0], w_out[0], b_out[0])
    out = _combine_call(tiled(ctm), ys, x1, gate_t, gt2, g_final[None, :], t // ctm, ctm)
    return out.reshape(b, t, dm)
```

```python
import functools

import jax
import jax.numpy as jnp
from jax import lax
from jax.experimental import pallas as pl
from jax.experimental.pallas import tpu as pltpu

F32 = jnp.float32
BF16 = jnp.bfloat16

HEAD_DIM = 64
GQA_HEADS = 8
GQA_KV_HEADS = 2
GQA_GROUP = GQA_HEADS // GQA_KV_HEADS
DIFF_HEADS = 4
DIFF_V_DIM = 2 * HEAD_DIM
GRID_W = 64
ROPE_THETA = 10000.0
N_EXPERTS = 32
TOP_K = 4
SWIGLU_ALPHA = 1.702
SWIGLU_LIMIT = 7.0
N_MOD = 6
EPS = 1e-6
LAMBDA_INIT = 0.8 - 0.6
LOG2E = 1.4426950408889634
Q_SCALE = HEAD_DIM ** -0.5 * LOG2E

LANES = 128
SUBLANES = 8
KV_COLS = 2 * GQA_KV_HEADS * HEAD_DIM + DIFF_HEADS * 2 * HEAD_DIM + DIFF_HEADS * DIFF_V_DIM
Q_COLS = GQA_HEADS * HEAD_DIM + DIFF_HEADS * 2 * HEAD_DIM

PROJ_TM = 256
GQA_TQ = 128
DIFF_TQ = 256
ATT_TK = 2816
MXU_TILE = 256
OPROJ_TM = 256
DISPATCH_TM = 512
COMBINE_TM = 256
EXPERT_BLK = 512


def _store_rows(ref, value):
    m = value.shape[0]
    for s in range(SUBLANES):
        ref[pl.ds(s, m, stride=SUBLANES), :] = value[:, s * LANES:(s + 1) * LANES]


def _load_rows(ref):
    m = ref.shape[0] // SUBLANES
    return jnp.concatenate([ref[pl.ds(s, m, stride=SUBLANES), :] for s in range(SUBLANES)], axis=1)


def _cparams(n_axes, vmem_mb):
    return pltpu.CompilerParams(dimension_semantics=("arbitrary",) * n_axes,
                                vmem_limit_bytes=vmem_mb * 1024 * 1024)


def _mod_kernel(cv_ref, w_ref, b_ref, o_ref):
    cv = cv_ref[...]
    s = cv * (1.0 / (1.0 + jnp.exp(-cv)))
    o_ref[...] = jnp.dot(s, w_ref[...], preferred_element_type=F32,
                         precision=lax.Precision.HIGHEST) + b_ref[...]


def _mod_call(cv, w_ada, b_ada):
    d, n = w_ada.shape
    tn = 1024
    return pl.pallas_call(
        _mod_kernel,
        grid=(n // tn,),
        in_specs=[pl.BlockSpec((8, d), lambda j: (0, 0)),
                  pl.BlockSpec((d, tn), lambda j: (0, j)),
                  pl.BlockSpec((1, tn), lambda j: (0, j))],
        out_specs=pl.BlockSpec((8, tn), lambda j: (0, j)),
        out_shape=jax.ShapeDtypeStruct((8, n), F32),
        compiler_params=_cparams(1, 32),
    )(cv, w_ada, b_ada.reshape(1, n))


def _head_rms(v, gain, bd):
    sq = v * v
    hi = sq.astype(BF16)
    lo = (sq - hi.astype(F32)).astype(BF16)
    ss = (jnp.dot(hi, bd, preferred_element_type=F32) + jnp.dot(lo, bd, preferred_element_type=F32))
    return v * lax.rsqrt(ss * (1.0 / HEAD_DIM) + EPS) * gain


def _rope_chunk(v, cos, sin, first_half):
    partner = jnp.where(first_half, pltpu.roll(v, LANES - 16, 1), pltpu.roll(v, 16, 1))
    return v * cos + partner * sin


def _proj_kernel(x_ref, sh_ref, sc_ref, g_ref, w_ref, cos_ref, sin_ref, gq_ref, gk_ref, bd_ref,
                 *out_refs, rope, with_q):
    xf = x_ref[0]
    ms = jnp.mean(xf * xf, axis=-1, keepdims=True)
    h = xf * lax.rsqrt(ms + EPS) * g_ref[...]
    h = h * (1.0 + sc_ref[0]) + sh_ref[0]
    p = jnp.dot(h.astype(BF16), w_ref[...], preferred_element_type=F32)

    if with_q:
        ka_ref, va_ref, kb_ref, vb_ref, qa_ref, qb_ref = out_refs
    else:
        ka_ref, va_ref, kb_ref, vb_ref = out_refs
    bd = bd_ref[...]
    if rope:
        cos = cos_ref[...]
        sin = sin_ref[...]
        lane = lax.broadcasted_iota(jnp.int32, cos.shape, 1)
        first_half = (lane % 32) < 16

    def chunk(j):
        return p[:, j * LANES:(j + 1) * LANES]

    def maybe_rope(v):
        return _rope_chunk(v, cos, sin, first_half) if rope else v

    def put_heads(ref, j, v):
        ref[0, 2 * j] = v[:, :HEAD_DIM].astype(ref.dtype)
        ref[0, 2 * j + 1] = v[:, HEAD_DIM:].astype(ref.dtype)

    put_heads(ka_ref, 0, maybe_rope(_head_rms(chunk(0), gk_ref[...], bd)))
    put_heads(va_ref, 0, chunk(1))
    for j in range(4):
        put_heads(kb_ref, j, maybe_rope(chunk(2 + j)))
        vb_ref[0, j] = chunk(6 + j).astype(vb_ref.dtype)
    if with_q:
        for j in range(4):
            put_heads(qa_ref, j, maybe_rope(_head_rms(chunk(10 + j), gq_ref[...], bd)) * Q_SCALE)
            put_heads(qb_ref, j, maybe_rope(chunk(14 + j)) * Q_SCALE)


def _proj_call(x, shift, scale, g, w, cos, sin, gq, gk, bd, *, rope, with_q):
    nb, t, d = x.shape
    tm = min(PROJ_TM, t)
    ncols = w.shape[1]
    head = lambda n: pl.BlockSpec((1, n, tm, HEAD_DIM), lambda b, i: (b, 0, i, 0))
    out_specs = [head(2), head(2), head(8), pl.BlockSpec((1, 4, tm, DIFF_V_DIM), lambda b, i: (b, 0, i, 0))]
    out_shape = [jax.ShapeDtypeStruct((nb, 2, t, HEAD_DIM), BF16),
                 jax.ShapeDtypeStruct((nb, 2, t, HEAD_DIM), BF16),
                 jax.ShapeDtypeStruct((nb, 8, t, HEAD_DIM), BF16),
                 jax.ShapeDtypeStruct((nb, 4, t, DIFF_V_DIM), BF16)]
    if with_q:
        out_specs += [head(8), head(8)]
        out_shape += [jax.ShapeDtypeStruct((nb, 8, t, HEAD_DIM), BF16)] * 2
    const = lambda shape: pl.BlockSpec(shape, lambda b, i: (0,) * len(shape))
    return pl.pallas_call(
        functools.partial(_proj_kernel, rope=rope, with_q=with_q),
        grid=(nb, t // tm),
        in_specs=[pl.BlockSpec((1, tm, d), lambda b, i: (b, i, 0)),
                  pl.BlockSpec((1, 1, d), lambda b, i: (b, 0, 0)),
                  pl.BlockSpec((1, 1, d), lambda b, i: (b, 0, 0)),
                  const((1, d)),
                  const((d, ncols)),
                  pl.BlockSpec((tm, LANES), lambda b, i: (i, 0)),
                  pl.BlockSpec((tm, LANES), lambda b, i: (i, 0)),
                  const((1, LANES)), const((1, LANES)), const((LANES, LANES))],
        out_specs=out_specs,
        out_shape=out_shape,
        compiler_params=_cparams(2, 48),
    )(x, shift, scale, g, w, cos, sin, gq, gk, bd)


def _scores(q, k):
    return lax.dot_general(q, k, (((1,), (1,)), ((), ())), preferred_element_type=F32)


def _softmax_step(s, v, carry):
    m, l, acc = carry
    m_new = jnp.maximum(m, jnp.max(s, axis=1, keepdims=True))
    alpha = jnp.exp2(m - m_new)
    p = jnp.exp2(s - m_new)
    l_new = alpha * l + jnp.sum(p, axis=1, keepdims=True)
    acc_new = alpha * acc + jnp.dot(p.astype(BF16), v, preferred_element_type=F32)
    return m_new, l_new, acc_new


def _init_carry(m_rows, dv):
    return (jnp.full((m_rows, 1), -jnp.inf, F32), jnp.zeros((m_rows, 1), F32), jnp.zeros((m_rows, dv), F32))


def _key_tile(n_keys):
    return max(tk for tk in range(MXU_TILE, min(ATT_TK, n_keys) + 1, MXU_TILE) if n_keys % tk == 0)


def _attend(scores, k_count, v_ref, m_rows, dv, tk):
    carry = _init_carry(m_rows, dv)
    for j in range(k_count // tk):
        rows = pl.ds(j * tk, tk)
        carry = _softmax_step(scores(rows), v_ref[0, 0, rows, :], carry)
    _, l, acc = carry
    return acc / l


def _gqa_kernel(q_ref, k_ref, v_ref, o_ref, *, tk):
    tq = q_ref.shape[2]
    q = q_ref[0].reshape(GQA_GROUP * tq, HEAD_DIM)
    o = _attend(lambda rows: _scores(q, k_ref[0, 0, rows, :]), k_ref.shape[2], v_ref,
                GQA_GROUP * tq, HEAD_DIM, tk).reshape(GQA_GROUP, tq, HEAD_DIM)
    for g in range(GQA_GROUP):
        o_ref[0, :, g * HEAD_DIM:(g + 1) * HEAD_DIM] = o[g].astype(o_ref.dtype)


def _gqa_call(qa, ka, va):
    b, _, t, _ = qa.shape
    nk = ka.shape[2]
    tq = min(GQA_TQ, t)
    kv = pl.BlockSpec((1, 1, nk, HEAD_DIM), lambda bi, h, i: (bi, h, 0, 0))
    return pl.pallas_call(
        functools.partial(_gqa_kernel, tk=_key_tile(nk)),
        grid=(b, GQA_KV_HEADS, t // tq),
        in_specs=[pl.BlockSpec((1, GQA_GROUP, tq, HEAD_DIM), lambda bi, h, i: (bi, h, i, 0)), kv, kv],
        out_specs=pl.BlockSpec((1, tq, GQA_GROUP * HEAD_DIM), lambda bi, h, i: (bi, i, h)),
        out_shape=jax.ShapeDtypeStruct((b, t, GQA_HEADS * HEAD_DIM), BF16),
        compiler_params=_cparams(3, 48),
    )(qa, ka, va)


def _diff_kernel(q_ref, k_ref, v_ref, lam_ref, sg_ref, o_ref, *, tk):
    tq = q_ref.shape[2]
    q = (q_ref[0, 0], q_ref[0, 1])

    def scores(rows):
        return jnp.concatenate([_scores(q[c], k_ref[0, c, rows, :]) for c in range(2)], axis=0)

    o_both = _attend(scores, k_ref.shape[2], v_ref, 2 * tq, DIFF_V_DIM, tk)
    lamf = lam_ref[...]
    lam = (jnp.exp(jnp.sum(lamf[0:1] * lamf[1:2], axis=-1, keepdims=True))
           - jnp.exp(jnp.sum(lamf[2:3] * lamf[3:4], axis=-1, keepdims=True)) + LAMBDA_INIT)
    o = o_both[:tq] - lam * o_both[tq:]
    ms = jnp.mean(o * o, axis=-1, keepdims=True)
    o = o * lax.rsqrt(ms + EPS) * sg_ref[...] * (1.0 - LAMBDA_INIT)
    o_ref[0] = o.astype(o_ref.dtype)


def _diff_call(qb, kb, vb, lam, sub_g):
    b, _, t, _ = qb.shape
    nk = kb.shape[2]
    tq = min(DIFF_TQ, t)
    return pl.pallas_call(
        functools.partial(_diff_kernel, tk=_key_tile(nk)),
        grid=(b, DIFF_HEADS, t // tq),
        in_specs=[pl.BlockSpec((1, 2, tq, HEAD_DIM), lambda bi, h, i: (bi, h, i, 0)),
                  pl.BlockSpec((1, 2, nk, HEAD_DIM), lambda bi, h, i: (bi, h, 0, 0)),
                  pl.BlockSpec((1, 1, nk, DIFF_V_DIM), lambda bi, h, i: (bi, h, 0, 0)),
                  pl.BlockSpec((4, HEAD_DIM), lambda bi, h, i: (0, 0)),
                  pl.BlockSpec((1, DIFF_V_DIM), lambda bi, h, i: (0, 0))],
        out_specs=pl.BlockSpec((1, tq, DIFF_V_DIM), lambda bi, h, i: (bi, i, h)),
        out_shape=jax.ShapeDtypeStruct((b, t, DIFF_HEADS * DIFF_V_DIM), BF16),
        compiler_params=_cparams(3, 48),
    )(qb, kb, vb, lam, sub_g)


def _oproj_kernel(a_ref, d_ref, x_ref, gt_ref, sh_ref, sc_ref, g_ref, wo_ref, wr_ref, br_ref, tri_ref,
                  x1_ref, h2_ref, idx_ref, rank_ref, gate_ref, cnt_ref):
    i = pl.program_id(0)

    @pl.when(i == 0)
    def _():
        cnt_ref[...] = jnp.zeros_like(cnt_ref)

    half = a_ref.shape[1]
    mix = (jnp.dot(a_ref[...], wo_ref[0:half, :], preferred_element_type=F32)
           + jnp.dot(d_ref[...], wo_ref[half:2 * half, :], preferred_element_type=F32))
    x1 = x_ref[...] + gt_ref[0] * mix
    x1_ref[...] = x1
    ms = jnp.mean(x1 * x1, axis=-1, keepdims=True)
    h2 = x1 * lax.rsqrt(ms + EPS) * g_ref[...]
    h2 = h2 * (1.0 + sc_ref[0]) + sh_ref[0]
    _store_rows(h2_ref, h2)

    logits = jnp.dot(h2, wr_ref[...], preferred_element_type=F32,
                     precision=lax.Precision.HIGHEST) + br_ref[...]
    lt = logits.T[0:N_EXPERTS, :]
    tm = lt.shape[1]
    eid = lax.broadcasted_iota(jnp.int32, lt.shape, 0).astype(F32)
    vals, sels = [], []
    for k in range(TOP_K):
        m = jnp.max(lt, axis=0, keepdims=True)
        first = jnp.min(jnp.where(lt == m, eid, float(N_EXPERTS)), axis=0, keepdims=True)
        sel = eid == first
        lt = jnp.where(sel, -jnp.inf, lt)
        vals.append(m)
        sels.append(sel)
        idx_ref[k:k + 1, :] = first.astype(jnp.int32)

    onehot = sum(s.astype(F32) for s in sels)
    before = cnt_ref[:, 0:1] + jnp.dot(onehot.astype(BF16), tri_ref[...], preferred_element_type=F32)
    for k in range(TOP_K):
        rank_ref[k:k + 1, :] = jnp.sum(jnp.where(sels[k], before, 0.0), axis=0, keepdims=True).astype(jnp.int32)
    cnt_ref[...] = cnt_ref[...] + jnp.sum(onehot, axis=1, keepdims=True)

    es = [jnp.exp(v - vals[0]) for v in vals]
    den = es[0] + es[1] + es[2] + es[3]
    gates = jnp.concatenate([e / den for e in es] + [jnp.zeros((LANES - TOP_K, tm), F32)], axis=0)
    gate_ref[...] = gates.T


def _oproj_call(a, d, x, gt1, sh2, sc2, g_ffn, w_o, wr_pad, br_pad, tiles_per_batch):
    n, dm = x.shape
    tm = OPROJ_TM
    tri = (lax.broadcasted_iota(jnp.int32, (tm, tm), 0) < lax.broadcasted_iota(jnp.int32, (tm, tm), 1)).astype(BF16)
    row = lambda w: pl.BlockSpec((tm, w), lambda i: (i, 0))
    per_batch = pl.BlockSpec((1, 1, dm), lambda i: (i // tiles_per_batch, 0, 0))
    const = lambda shape: pl.BlockSpec(shape, lambda i: (0,) * len(shape))
    return pl.pallas_call(
        _oproj_kernel,
        grid=(n // tm,),
        in_specs=[row(a.shape[1]), row(d.shape[1]), row(dm), per_batch, per_batch, per_batch,
                  const((1, dm)), const(w_o.shape), const(wr_pad.shape), const((1, LANES)), const((tm, tm))],
        out_specs=[row(dm), pl.BlockSpec((tm * SUBLANES, LANES), lambda i: (i, 0)),
                   pl.BlockSpec((TOP_K, tm), lambda i: (0, i)),
                   pl.BlockSpec((TOP_K, tm), lambda i: (0, i)),
                   row(LANES),
                   const((N_EXPERTS, LANES))],
        out_shape=[jax.ShapeDtypeStruct((n, dm), F32),
                   jax.ShapeDtypeStruct((n * SUBLANES, LANES), F32),
                   jax.ShapeDtypeStruct((TOP_K, n), jnp.int32),
                   jax.ShapeDtypeStruct((TOP_K, n), jnp.int32),
                   jax.ShapeDtypeStruct((n, LANES), F32),
                   jax.ShapeDtypeStruct((N_EXPERTS, LANES), F32)],
        compiler_params=_cparams(1, 48),
    )(a, d, x, gt1, sh2, sc2, g_ffn, w_o, wr_pad, br_pad, tri)


def _row_copy(src_ref, src_row, dst_ref, dst_row, sem):
    tile = lambda r: pl.ds(pl.multiple_of(r * SUBLANES, SUBLANES), SUBLANES)
    return pltpu.make_async_copy(src_ref.at[tile(src_row)], dst_ref.at[tile(dst_row)], sem)


def _dispatch_kernel(cnt_ref, pad_ref, start_ref, nused_ref, dest_hbm, h2_ref, xs_hbm,
                     idx_smem, zeros, sem_i, sem_d, sem_z, *, tm, blk):
    i = pl.program_id(0)

    @pl.when(i == 0)
    def _():
        zeros[...] = jnp.zeros_like(zeros)
        for e in range(N_EXPERTS):
            lo = start_ref[e] + cnt_ref[e]
            hi = start_ref[e] + pad_ref[e]
            lax.fori_loop(lo, hi, lambda r, c: (_row_copy(zeros, 0, xs_hbm, r, sem_z).start(), c)[1], 0)
        for e in range(N_EXPERTS):
            lo = start_ref[e] + cnt_ref[e]
            hi = start_ref[e] + pad_ref[e]
            lax.fori_loop(lo, hi, lambda r, c: (_row_copy(zeros, 0, xs_hbm, r, sem_z).wait(), c)[1], 0)

        def tail_copy(j):
            rows = pl.ds(pl.multiple_of(j * (blk * SUBLANES), blk * SUBLANES), blk * SUBLANES)
            return pltpu.make_async_copy(zeros, xs_hbm.at[rows], sem_z)

        n_blocks = xs_hbm.shape[0] // (blk * SUBLANES)
        lax.fori_loop(nused_ref[0], n_blocks, lambda j, c: (tail_copy(j).start(), c)[1], 0)
        lax.fori_loop(nused_ref[0], n_blocks, lambda j, c: (tail_copy(j).wait(), c)[1], 0)

    idx_copy = pltpu.make_async_copy(dest_hbm.at[pl.ds(i * (TOP_K * tm), TOP_K * tm)], idx_smem, sem_i)
    idx_copy.start()
    idx_copy.wait()

    def issue(t, c):
        for k in range(TOP_K):
            _row_copy(h2_ref, t, xs_hbm, idx_smem[k * tm + t], sem_d).start(priority=k % 2)
        return c

    def drain(t, c):
        for k in range(TOP_K):
            _row_copy(h2_ref, 0, xs_hbm, 0, sem_d).wait()
        return c

    lax.fori_loop(0, tm, issue, 0)
    lax.fori_loop(0, tm, drain, 0)


def _dispatch_call(counts, padded, starts, n_used, dest_flat, h2, n_rows, tm):
    n = h2.shape[0] // SUBLANES
    any_spec = pl.BlockSpec(memory_space=pl.ANY)
    return pl.pallas_call(
        functools.partial(_dispatch_kernel, tm=tm, blk=EXPERT_BLK),
        grid_spec=pltpu.PrefetchScalarGridSpec(
            num_scalar_prefetch=4,
            grid=(n // tm,),
            in_specs=[any_spec, pl.BlockSpec((tm * SUBLANES, LANES), lambda i, *_: (i, 0))],
            out_specs=any_spec,
            scratch_shapes=[pltpu.SMEM((TOP_K * tm,), jnp.int32),
                            pltpu.VMEM((EXPERT_BLK * SUBLANES, LANES), h2.dtype),
                            pltpu.SemaphoreType.DMA(()), pltpu.SemaphoreType.DMA(()),
                            pltpu.SemaphoreType.DMA(())]),
        out_shape=jax.ShapeDtypeStruct((n_rows * SUBLANES, LANES), h2.dtype),
        compiler_params=_cparams(1, 24),
    )(counts, padded, starts, n_used, dest_flat, h2)


def _experts_kernel(bexp_ref, nused_ref, xs_ref, win_ref, bin_ref, wout_ref, bout_ref, ys_ref, win_bf, wout_bf):
    i = pl.program_id(0)
    prev = bexp_ref[jnp.maximum(i - 1, 0)]
    new_expert = jnp.logical_or(i == 0, bexp_ref[i] != prev)

    @pl.when(jnp.logical_and(i < nused_ref[0], new_expert))
    def _():
        win_bf[...] = win_ref[0].astype(BF16)
        wout_bf[...] = wout_ref[0].astype(BF16)

    @pl.when(i < nused_ref[0])
    def _():
        dff = wout_bf.shape[0]
        hcat = jnp.dot(_load_rows(xs_ref).astype(BF16), win_bf[...], preferred_element_type=F32) + bin_ref[0]
        x_glu = jnp.minimum(hcat[:, :dff], SWIGLU_LIMIT)
        x_lin = jnp.clip(hcat[:, dff:], -SWIGLU_LIMIT, SWIGLU_LIMIT)
        act = x_glu * (1.0 / (1.0 + jnp.exp(-SWIGLU_ALPHA * x_glu))) * (x_lin + 1.0)
        _store_rows(ys_ref, jnp.dot(act.astype(BF16), wout_bf[...], preferred_element_type=F32) + bout_ref[0])

    @pl.when(i >= nused_ref[0])
    def _():
        ys_ref[...] = jnp.zeros_like(ys_ref)


def _experts_call(block_exp, n_used, xs, w_in, b_in, w_out, b_out):
    n_rows = xs.shape[0] // SUBLANES
    ne, dm, dff2 = w_in.shape
    dff = dff2 // 2
    blk = EXPERT_BLK
    rows = pl.BlockSpec((blk * SUBLANES, LANES), lambda i, be, nu: (jnp.minimum(i, nu[0] - 1), 0))
    return pl.pallas_call(
        _experts_kernel,
        grid_spec=pltpu.PrefetchScalarGridSpec(
            num_scalar_prefetch=2,
            grid=(n_rows // blk,),
            in_specs=[rows,
                      pl.BlockSpec((1, dm, dff2), lambda i, be, nu: (be[i], 0, 0)),
                      pl.BlockSpec((1, 1, dff2), lambda i, be, nu: (be[i], 0, 0)),
                      pl.BlockSpec((1, dff, dm), lambda i, be, nu: (be[i], 0, 0)),
                      pl.BlockSpec((1, 1, dm), lambda i, be, nu: (be[i], 0, 0))],
            out_specs=pl.BlockSpec((blk * SUBLANES, LANES), lambda i, be, nu: (i, 0)),
            scratch_shapes=[pltpu.VMEM((dm, dff2), BF16), pltpu.VMEM((dff, dm), BF16)]),
        out_shape=jax.ShapeDtypeStruct((n_rows * SUBLANES, LANES), F32),
        compiler_params=_cparams(1, 56),
    )(block_exp, n_used, xs, w_in, b_in.reshape(ne, 1, dff2), w_out, b_out.reshape(ne, 1, dm))


def _combine_kernel(dest_hbm, ys_hbm, x1_ref, gate_ref, gt_ref, gf_ref, o_ref, idx_smem, rows, sem_i, sem_d, *, tm):
    i = pl.program_id(0)

    def fetch(step, slot):
        idx_copy = pltpu.make_async_copy(dest_hbm.at[pl.ds(step * (TOP_K * tm), TOP_K * tm)], idx_smem, sem_i)
        idx_copy.start()
        idx_copy.wait()

        def issue(t, c):
            for k in range(TOP_K):
                _row_copy(ys_hbm, idx_smem[k * tm + t], rows.at[slot, k], t, sem_d.at[slot]).start(priority=k % 2)
            return c

        lax.fori_loop(0, tm, issue, 0)

    @pl.when(i == 0)
    def _():
        fetch(0, 0)

    for parity in range(2):
        @pl.when(jnp.logical_and(i + 1 < pl.num_programs(0), (i + 1) % 2 == parity))
        def _():
            fetch(i + 1, parity)

    slot = i % 2

    def drain(t, c):
        for k in range(TOP_K):
            _row_copy(ys_hbm, 0, rows.at[slot, k], 0, sem_d.at[slot]).wait()
        return c

    lax.fori_loop(0, tm, drain, 0)

    gates = gate_ref[...]
    y = gates[:, 0:1] * _load_rows(rows.at[slot, 0])
    for k in range(1, TOP_K):
        y = y + gates[:, k:k + 1] * _load_rows(rows.at[slot, k])
    x2 = x1_ref[...] + gt_ref[0] * y
    ms = jnp.mean(x2 * x2, axis=-1, keepdims=True)
    o_ref[...] = x2 * lax.rsqrt(ms + EPS) * gf_ref[...]


def _combine_call(dest_flat, ys, x1, gate_t, gt2, g_final, tiles_per_batch, tm):
    n, dm = x1.shape
    row = lambda w: pl.BlockSpec((tm, w), lambda i: (i, 0))
    return pl.pallas_call(
        functools.partial(_combine_kernel, tm=tm),
        grid=(n // tm,),
        in_specs=[pl.BlockSpec(memory_space=pl.ANY), pl.BlockSpec(memory_space=pl.ANY),
                  row(dm), row(LANES),
                  pl.BlockSpec((1, 1, dm), lambda i: (i // tiles_per_batch, 0, 0)),
                  pl.BlockSpec((1, dm), lambda i: (0, 0))],
        out_specs=row(dm),
        out_shape=jax.ShapeDtypeStruct((n, dm), F32),
        scratch_shapes=[pltpu.SMEM((TOP_K * tm,), jnp.int32),
                        pltpu.VMEM((2, TOP_K, tm * SUBLANES, LANES), F32),
                        pltpu.SemaphoreType.DMA(()), pltpu.SemaphoreType.DMA((2,))],
        compiler_params=_cparams(1, 40),
    )(dest_flat, ys, x1, gate_t, gt2, g_final)


def _rope_tables(t):
    rows = t // GRID_W
    n_freq = HEAD_DIM // 4
    inv_freq = ROPE_THETA ** (-jnp.arange(n_freq, dtype=F32) / n_freq)
    ar = jnp.arange(rows, dtype=F32)[:, None] * inv_freq
    ac = jnp.arange(GRID_W, dtype=F32)[:, None] * inv_freq
    per_row = lambda a: jnp.broadcast_to(a[:, None, :], (rows, GRID_W, n_freq))
    per_col = lambda a: jnp.broadcast_to(a[None, :, :], (rows, GRID_W, n_freq))
    cr, sr, cc, sc = per_row(jnp.cos(ar)), per_row(jnp.sin(ar)), per_col(jnp.cos(ac)), per_col(jnp.sin(ac))
    cos = jnp.concatenate([cr, cr, cc, cc] * 2, axis=-1).reshape(t, LANES)
    sin = jnp.concatenate([-sr, sr, -sc, sc] * 2, axis=-1).reshape(t, LANES)
    return cos, sin


def kernel(x, c, ctx, c_ctx, w_ada, b_ada, g_attn, w_qkv, gqa_q_norm, gqa_k_norm, diff_lambda,
           diff_subln, w_o, g_ffn, w_router, b_router, w_in, b_in, w_out, b_out, g_final):
    assert w_ada.shape[0] == 1, "single-layer block"
    b, t, dm = x.shape
    n = b * t

    cv = jnp.zeros((8, dm), F32).at[:b].set(c).at[b].set(c_ctx)
    mod = _mod_call(cv, w_ada[0], b_ada[0])
    sh1, sc1, gt1, sh2, sc2, gt2 = [m[:b, None, :] for m in jnp.split(mod, N_MOD, axis=-1)]
    csh1, csc1 = [jnp.broadcast_to(m[b][None, None, :], (b, 1, dm)) for m in jnp.split(mod, N_MOD, axis=-1)[:2]]

    wq = w_qkv[0]
    o_ka, o_va, o_qb, o_kb, o_vb = 512, 640, 768, 1280, 1792
    w_kv = jnp.concatenate([wq[:, o_ka:o_qb], wq[:, o_kb:]], axis=1)
    w_all = jnp.concatenate([w_kv, wq[:, :o_ka], wq[:, o_qb:o_kb]], axis=1).astype(BF16)
    gq = jnp.tile(gqa_q_norm[0], 2)[None, :]
    gk = jnp.tile(gqa_k_norm[0], 2)[None, :]
    lane = jnp.arange(LANES)
    bd = (lane[:, None] // HEAD_DIM == lane[None, :] // HEAD_DIM).astype(BF16)
    cos, sin = _rope_tables(t)
    g_a = g_attn[0][None, :]

    ka, va, kb, vb, qa, qb = _proj_call(x, sh1, sc1, g_a, w_all, cos, sin, gq, gk, bd, rope=True, with_q=True)
    ka_c, va_c, kb_c, vb_c = _proj_call(ctx, csh1, csc1, g_a, w_all[:, :KV_COLS], cos, sin, gq, gk, bd,
                                        rope=False, with_q=False)

    keys = lambda ctx_part, lat_part: jnp.concatenate([ctx_part, lat_part], axis=2)
    att_a = _gqa_call(qa, keys(ka_c, ka), keys(va_c, va))
    att_d = _diff_call(qb, keys(kb_c, kb), keys(vb_c, vb), diff_lambda[0], diff_subln[0][None, :])

    wr_pad = jnp.zeros((dm, LANES), F32).at[:, :N_EXPERTS].set(w_router[0])
    br_pad = jnp.zeros((1, LANES), F32).at[0, :N_EXPERTS].set(b_router[0])
    x1, h2, idx, rank, gate_t, cnt = _oproj_call(
        att_a.reshape(n, -1), att_d.reshape(n, -1), x.reshape(n, dm), gt1, sh2, sc2, g_ffn[0][None, :],
        w_o[0].astype(BF16), wr_pad, br_pad, t // OPROJ_TM)

    blk = EXPERT_BLK
    counts = cnt[:, 0].astype(jnp.int32)
    padded = (counts + blk - 1) // blk * blk
    pad_ends = jnp.cumsum(padded)
    starts = pad_ends - padded
    n_blocks = n * TOP_K // blk + N_EXPERTS
    expert_ids = jnp.arange(N_EXPERTS, dtype=jnp.int32)
    dest = rank + jnp.sum(jnp.where(idx[..., None] == expert_ids, starts, 0), axis=-1)
    tiled = lambda tm: dest.reshape(TOP_K, n // tm, tm).transpose(1, 0, 2).reshape(-1)
    block_row0 = jnp.arange(n_blocks, dtype=jnp.int32) * blk
    block_exp = jnp.minimum(jnp.sum((pad_ends[None, :] <= block_row0[:, None]).astype(jnp.int32), axis=1),
                            N_EXPERTS - 1)
    n_used = (pad_ends[-1:] // blk).astype(jnp.int32)

    dtm = min(DISPATCH_TM, t)
    ctm = min(COMBINE_TM, t)
    xs = _dispatch_call(counts, padded, starts, n_used, tiled(dtm), h2, n_blocks * blk, dtm)
    ys = _experts_call(block_exp, n_used, xs, w_in[0], b_in[0], w_out[0], b_out[0])
    out = _combine_call(tiled(ctm), ys, x1, gate_t, gt2, g_final[None, :], t // ctm, ctm)
    return out.reshape(b, t, dm)
```

```python
import functools

import jax
import jax.numpy as jnp
from jax import lax
from jax.experimental import pallas as pl
from jax.experimental.pallas import tpu as pltpu

F32 = jnp.float32
BF16 = jnp.bfloat16

HEAD_DIM = 64
GQA_HEADS = 8
GQA_KV_HEADS = 2
GQA_GROUP = GQA_HEADS // GQA_KV_HEADS
DIFF_HEADS = 4
DIFF_V_DIM = 2 * HEAD_DIM
GRID_W = 64
ROPE_THETA = 10000.0
N_EXPERTS = 32
TOP_K = 4
SWIGLU_ALPHA = 1.702
SWIGLU_LIMIT = 7.0
N_MOD = 6
EPS = 1e-6
LAMBDA_INIT = 0.8 - 0.6
LOG2E = 1.4426950408889634
Q_SCALE = HEAD_DIM ** -0.5 * LOG2E

LANES = 128
SUBLANES = 8
GQA_V_EXT = LANES
KV_COLS = 2 * GQA_KV_HEADS * HEAD_DIM + DIFF_HEADS * 2 * HEAD_DIM + DIFF_HEADS * DIFF_V_DIM
Q_COLS = GQA_HEADS * HEAD_DIM + DIFF_HEADS * 2 * HEAD_DIM

PROJ_TM = 512
GQA_TQ = 256
DIFF_TQ = 512
ATT_TK = 2816
MXU_TILE = 256
OPROJ_TM = 512
DISPATCH_TM = 512
COMBINE_TM = 256
EXPERT_BLK = 512


def _store_rows(ref, value):
    m = value.shape[0]
    for s in range(SUBLANES):
        ref[pl.ds(s, m, stride=SUBLANES), :] = value[:, s * LANES:(s + 1) * LANES]


def _load_rows(ref):
    m = ref.shape[0] // SUBLANES
    return jnp.concatenate([ref[pl.ds(s, m, stride=SUBLANES), :] for s in range(SUBLANES)], axis=1)


def _cparams(n_axes, vmem_mb):
    return pltpu.CompilerParams(dimension_semantics=("arbitrary",) * n_axes,
                                vmem_limit_bytes=vmem_mb * 1024 * 1024)


def _mod_kernel(cv_ref, w_ref, b_ref, o_ref):
    cv = cv_ref[...]
    s = cv * (1.0 / (1.0 + jnp.exp(-cv)))
    o_ref[...] = jnp.dot(s, w_ref[...], preferred_element_type=F32,
                         precision=lax.Precision.HIGHEST) + b_ref[...]


def _mod_call(cv, w_ada, b_ada):
    d, n = w_ada.shape
    tn = 1024
    return pl.pallas_call(
        _mod_kernel,
        grid=(n // tn,),
        in_specs=[pl.BlockSpec((8, d), lambda j: (0, 0)),
                  pl.BlockSpec((d, tn), lambda j: (0, j)),
                  pl.BlockSpec((1, tn), lambda j: (0, j))],
        out_specs=pl.BlockSpec((8, tn), lambda j: (0, j)),
        out_shape=jax.ShapeDtypeStruct((8, n), F32),
        compiler_params=_cparams(1, 32),
    )(cv, w_ada, b_ada.reshape(1, n))


def _head_rms(v, gain, bd):
    sq = v * v
    hi = sq.astype(BF16)
    lo = (sq - hi.astype(F32)).astype(BF16)
    ss = (jnp.dot(hi, bd, preferred_element_type=F32) + jnp.dot(lo, bd, preferred_element_type=F32))
    return v * lax.rsqrt(ss * (1.0 / HEAD_DIM) + EPS) * gain


def _rope_chunk(v, cos, sin, first_half):
    partner = jnp.where(first_half, pltpu.roll(v, LANES - 16, 1), pltpu.roll(v, 16, 1))
    return v * cos + partner * sin


def _proj_kernel(x_ref, sh_ref, sc_ref, g_ref, w_ref, cos_ref, sin_ref, gq_ref, gk_ref, bd_ref,
                 *out_refs, rope, with_q):
    xf = x_ref[0]
    ms = jnp.mean(xf * xf, axis=-1, keepdims=True)
    h = xf * lax.rsqrt(ms + EPS) * g_ref[...]
    h = h * (1.0 + sc_ref[0]) + sh_ref[0]
    p = jnp.dot(h.astype(BF16), w_ref[...], preferred_element_type=F32)

    if with_q:
        ka_ref, va_ref, kb_ref, vb_ref, qa_ref, qb_ref = out_refs
    else:
        ka_ref, va_ref, kb_ref, vb_ref = out_refs
    bd = bd_ref[...]
    if rope:
        cos = cos_ref[...]
        sin = sin_ref[...]
        lane = lax.broadcasted_iota(jnp.int32, cos.shape, 1)
        first_half = (lane % 32) < 16

    def chunk(j):
        return p[:, j * LANES:(j + 1) * LANES]

    def maybe_rope(v):
        return _rope_chunk(v, cos, sin, first_half) if rope else v

    def put_heads(ref, j, v):
        ref[0, 2 * j] = v[:, :HEAD_DIM].astype(ref.dtype)
        ref[0, 2 * j + 1] = v[:, HEAD_DIM:].astype(ref.dtype)

    put_heads(ka_ref, 0, maybe_rope(_head_rms(chunk(0), gk_ref[...], bd)))
    lanes = lax.broadcasted_iota(jnp.int32, (p.shape[0], LANES), 1)
    va = chunk(1)
    for h, v in enumerate((va, pltpu.roll(va, HEAD_DIM, 1))):
        va_ref[0, h] = jnp.where(lanes < HEAD_DIM, v, (lanes == HEAD_DIM).astype(F32)).astype(va_ref.dtype)
    for j in range(4):
        put_heads(kb_ref, j, maybe_rope(chunk(2 + j)))
        vb_ref[0, j] = chunk(6 + j).astype(vb_ref.dtype)
    if with_q:
        for j in range(4):
            put_heads(qa_ref, j, maybe_rope(_head_rms(chunk(10 + j), gq_ref[...], bd)) * Q_SCALE)
            put_heads(qb_ref, j, maybe_rope(chunk(14 + j)) * Q_SCALE)


def _proj_call(x, shift, scale, g, w, cos, sin, gq, gk, bd, *, rope, with_q):
    nb, t, d = x.shape
    tm = min(PROJ_TM, t)
    ncols = w.shape[1]
    heads = lambda n, w: pl.BlockSpec((1, n, tm, w), lambda b, i: (b, 0, i, 0))
    head = lambda n: heads(n, HEAD_DIM)
    out_specs = [head(2), heads(2, GQA_V_EXT), head(8), heads(4, DIFF_V_DIM)]
    out_shape = [jax.ShapeDtypeStruct((nb, 2, t, HEAD_DIM), BF16),
                 jax.ShapeDtypeStruct((nb, 2, t, GQA_V_EXT), BF16),
                 jax.ShapeDtypeStruct((nb, 8, t, HEAD_DIM), BF16),
                 jax.ShapeDtypeStruct((nb, 4, t, DIFF_V_DIM), BF16)]
    if with_q:
        out_specs += [head(8), head(8)]
        out_shape += [jax.ShapeDtypeStruct((nb, 8, t, HEAD_DIM), BF16)] * 2
    const = lambda shape: pl.BlockSpec(shape, lambda b, i: (0,) * len(shape))
    return pl.pallas_call(
        functools.partial(_proj_kernel, rope=rope, with_q=with_q),
        grid=(nb, t // tm),
        in_specs=[pl.BlockSpec((1, tm, d), lambda b, i: (b, i, 0)),
                  pl.BlockSpec((1, 1, d), lambda b, i: (b, 0, 0)),
                  pl.BlockSpec((1, 1, d), lambda b, i: (b, 0, 0)),
                  const((1, d)),
                  const((d, ncols)),
                  pl.BlockSpec((tm, LANES), lambda b, i: (i, 0)),
                  pl.BlockSpec((tm, LANES), lambda b, i: (i, 0)),
                  const((1, LANES)), const((1, LANES)), const((LANES, LANES))],
        out_specs=out_specs,
        out_shape=out_shape,
        compiler_params=_cparams(2, 48),
    )(x, shift, scale, g, w, cos, sin, gq, gk, bd)


def _scores(q, k):
    return lax.dot_general(q, k, (((1,), (1,)), ((), ())), preferred_element_type=F32)


def _softmax_step(s, v, carry, sum_on_mxu):
    m, l, acc = carry
    m_new = jnp.maximum(m, jnp.max(s, axis=1, keepdims=True))
    alpha = jnp.exp2(m - m_new)
    p = jnp.exp2(s - m_new)
    if not sum_on_mxu:
        l = alpha * l + jnp.sum(p, axis=1, keepdims=True)
    acc_new = alpha * acc + jnp.dot(p.astype(BF16), v, preferred_element_type=F32)
    return m_new, l, acc_new


def _init_carry(m_rows, dv):
    return (jnp.full((m_rows, 1), -jnp.inf, F32), jnp.zeros((m_rows, 1), F32), jnp.zeros((m_rows, dv), F32))


def _key_tile(n_keys):
    return max(tk for tk in range(MXU_TILE, min(ATT_TK, n_keys) + 1, MXU_TILE) if n_keys % tk == 0)


def _attend(scores, k_count, v_ref, m_rows, dv, tk):
    sum_on_mxu = v_ref.shape[3] > dv
    carry = _init_carry(m_rows, v_ref.shape[3])
    for j in range(k_count // tk):
        rows = pl.ds(j * tk, tk)
        carry = _softmax_step(scores(rows), v_ref[0, 0, rows, :], carry, sum_on_mxu)
    _, l, acc = carry
    return acc[:, :dv] / (acc[:, dv:dv + 1] if sum_on_mxu else l)


def _gqa_kernel(q_ref, k_ref, v_ref, o_ref, *, tk):
    tq = q_ref.shape[2]
    q = q_ref[0].reshape(GQA_GROUP * tq, HEAD_DIM)
    o = _attend(lambda rows: _scores(q, k_ref[0, 0, rows, :]), k_ref.shape[2], v_ref,
                GQA_GROUP * tq, HEAD_DIM, tk).reshape(GQA_GROUP, tq, HEAD_DIM)
    for g in range(GQA_GROUP):
        o_ref[0, :, g * HEAD_DIM:(g + 1) * HEAD_DIM] = o[g].astype(o_ref.dtype)


def _gqa_call(qa, ka, va):
    b, _, t, _ = qa.shape
    nk = ka.shape[2]
    tq = min(GQA_TQ, t)
    kv = lambda w: pl.BlockSpec((1, 1, nk, w), lambda bi, h, i: (bi, h, 0, 0))
    return pl.pallas_call(
        functools.partial(_gqa_kernel, tk=_key_tile(nk)),
        grid=(b, GQA_KV_HEADS, t // tq),
        in_specs=[pl.BlockSpec((1, GQA_GROUP, tq, HEAD_DIM), lambda bi, h, i: (bi, h, i, 0)),
                  kv(HEAD_DIM), kv(va.shape[3])],
        out_specs=pl.BlockSpec((1, tq, GQA_GROUP * HEAD_DIM), lambda bi, h, i: (bi, i, h)),
        out_shape=jax.ShapeDtypeStruct((b, t, GQA_HEADS * HEAD_DIM), BF16),
        compiler_params=_cparams(3, 48),
    )(qa, ka, va)


def _diff_kernel(q_ref, k_ref, v_ref, lam_ref, sg_ref, o_ref, *, tk):
    tq = q_ref.shape[2]
    q = (q_ref[0, 0], q_ref[0, 1])

    def scores(rows):
        return jnp.concatenate([_scores(q[c], k_ref[0, c, rows, :]) for c in range(2)], axis=0)

    o_both = _attend(scores, k_ref.shape[2], v_ref, 2 * tq, DIFF_V_DIM, tk)
    lamf = lam_ref[...]
    lam = (jnp.exp(jnp.sum(lamf[0:1] * lamf[1:2], axis=-1, keepdims=True))
           - jnp.exp(jnp.sum(lamf[2:3] * lamf[3:4], axis=-1, keepdims=True)) + LAMBDA_INIT)
    o = o_both[:tq] - lam * o_both[tq:]
    ms = jnp.mean(o * o, axis=-1, keepdims=True)
    o = o * lax.rsqrt(ms + EPS) * sg_ref[...] * (1.0 - LAMBDA_INIT)
    o_ref[0] = o.astype(o_ref.dtype)


def _diff_call(qb, kb, vb, lam, sub_g):
    b, _, t, _ = qb.shape
    nk = kb.shape[2]
    tq = min(DIFF_TQ, t)
    return pl.pallas_call(
        functools.partial(_diff_kernel, tk=_key_tile(nk)),
        grid=(b, DIFF_HEADS, t // tq),
        in_specs=[pl.BlockSpec((1, 2, tq, HEAD_DIM), lambda bi, h, i: (bi, h, i, 0)),
                  pl.BlockSpec((1, 2, nk, HEAD_DIM), lambda bi, h, i: (bi, h, 0, 0)),
                  pl.BlockSpec((1, 1, nk, vb.shape[3]), lambda bi, h, i: (bi, h, 0, 0)),
                  pl.BlockSpec((4, HEAD_DIM), lambda bi, h, i: (0, 0)),
                  pl.BlockSpec((1, DIFF_V_DIM), lambda bi, h, i: (0, 0))],
        out_specs=pl.BlockSpec((1, tq, DIFF_V_DIM), lambda bi, h, i: (bi, i, h)),
        out_shape=jax.ShapeDtypeStruct((b, t, DIFF_HEADS * DIFF_V_DIM), BF16),
        compiler_params=_cparams(3, 48),
    )(qb, kb, vb, lam, sub_g)


def _oproj_kernel(a_ref, d_ref, x_ref, gt_ref, sh_ref, sc_ref, g_ref, wo_ref, wr_ref, br_ref, tri_ref,
                  x1_ref, h2_ref, idx_ref, rank_ref, gate_ref, cnt_ref):
    i = pl.program_id(0)

    @pl.when(i == 0)
    def _():
        cnt_ref[...] = jnp.zeros_like(cnt_ref)

    half = a_ref.shape[1]
    mix = (jnp.dot(a_ref[...], wo_ref[0:half, :], preferred_element_type=F32)
           + jnp.dot(d_ref[...], wo_ref[half:2 * half, :], preferred_element_type=F32))
    x1 = x_ref[...] + gt_ref[0] * mix
    x1_ref[...] = x1
    ms = jnp.mean(x1 * x1, axis=-1, keepdims=True)
    h2 = x1 * lax.rsqrt(ms + EPS) * g_ref[...]
    h2 = h2 * (1.0 + sc_ref[0]) + sh_ref[0]
    _store_rows(h2_ref, h2)

    h_hi = h2.astype(BF16)
    h_lo = (h2 - h_hi.astype(F32)).astype(BF16)
    logits = (jnp.dot(h_hi, wr_ref[0], preferred_element_type=F32)
              + jnp.dot(h_lo, wr_ref[0], preferred_element_type=F32)
              + jnp.dot(h_hi, wr_ref[1], preferred_element_type=F32)) + br_ref[...]
    lt = logits.T[0:N_EXPERTS, :]
    tm = lt.shape[1]
    eid = lax.broadcasted_iota(jnp.int32, lt.shape, 0).astype(F32)
    vals, sels = [], []
    for k in range(TOP_K):
        m = jnp.max(lt, axis=0, keepdims=True)
        first = jnp.min(jnp.where(lt == m, eid, float(N_EXPERTS)), axis=0, keepdims=True)
        sel = eid == first
        lt = jnp.where(sel, -jnp.inf, lt)
        vals.append(m)
        sels.append(sel)
        idx_ref[k:k + 1, :] = first.astype(jnp.int32)

    onehot = sum(s.astype(F32) for s in sels)
    before = cnt_ref[:, 0:1] + jnp.dot(onehot.astype(BF16), tri_ref[...], preferred_element_type=F32)
    for k in range(TOP_K):
        rank_ref[k:k + 1, :] = jnp.sum(jnp.where(sels[k], before, 0.0), axis=0, keepdims=True).astype(jnp.int32)
    cnt_ref[...] = cnt_ref[...] + jnp.sum(onehot, axis=1, keepdims=True)

    es = [jnp.exp(v - vals[0]) for v in vals]
    den = es[0] + es[1] + es[2] + es[3]
    gates = jnp.concatenate([e / den for e in es] + [jnp.zeros((LANES - TOP_K, tm), F32)], axis=0)
    gate_ref[...] = gates.T


def _oproj_call(a, d, x, gt1, sh2, sc2, g_ffn, w_o, wr_pad, br_pad, tiles_per_batch):
    n, dm = x.shape
    tm = OPROJ_TM
    tri = (lax.broadcasted_iota(jnp.int32, (tm, tm), 0) < lax.broadcasted_iota(jnp.int32, (tm, tm), 1)).astype(BF16)
    row = lambda w: pl.BlockSpec((tm, w), lambda i: (i, 0))
    per_batch = pl.BlockSpec((1, 1, dm), lambda i: (i // tiles_per_batch, 0, 0))
    const = lambda shape: pl.BlockSpec(shape, lambda i: (0,) * len(shape))
    return pl.pallas_call(
        _oproj_kernel,
        grid=(n // tm,),
        in_specs=[row(a.shape[1]), row(d.shape[1]), row(dm), per_batch, per_batch, per_batch,
                  const((1, dm)), const(w_o.shape), const(wr_pad.shape), const((1, LANES)), const((tm, tm))],
        out_specs=[row(dm), pl.BlockSpec((tm * SUBLANES, LANES), lambda i: (i, 0)),
                   pl.BlockSpec((TOP_K, tm), lambda i: (0, i)),
                   pl.BlockSpec((TOP_K, tm), lambda i: (0, i)),
                   row(LANES),
                   const((N_EXPERTS, LANES))],
        out_shape=[jax.ShapeDtypeStruct((n, dm), F32),
                   jax.ShapeDtypeStruct((n * SUBLANES, LANES), F32),
                   jax.ShapeDtypeStruct((TOP_K, n), jnp.int32),
                   jax.ShapeDtypeStruct((TOP_K, n), jnp.int32),
                   jax.ShapeDtypeStruct((n, LANES), F32),
                   jax.ShapeDtypeStruct((N_EXPERTS, LANES), F32)],
        compiler_params=_cparams(1, 48),
    )(a, d, x, gt1, sh2, sc2, g_ffn, w_o, wr_pad, br_pad, tri)


def _row_copy(src_ref, src_row, dst_ref, dst_row, sem):
    tile = lambda r: pl.ds(pl.multiple_of(r * SUBLANES, SUBLANES), SUBLANES)
    return pltpu.make_async_copy(src_ref.at[tile(src_row)], dst_ref.at[tile(dst_row)], sem)


def _dispatch_kernel(cnt_ref, pad_ref, start_ref, nused_ref, dest_hbm, h2_ref, xs_hbm,
                     idx_smem, zeros, sem_i, sem_d, sem_z, *, tm, blk):
    i = pl.program_id(0)

    @pl.when(i == 0)
    def _():
        zeros[...] = jnp.zeros_like(zeros)
        for e in range(N_EXPERTS):
            lo = start_ref[e] + cnt_ref[e]
            hi = start_ref[e] + pad_ref[e]
            lax.fori_loop(lo, hi, lambda r, c: (_row_copy(zeros, 0, xs_hbm, r, sem_z).start(), c)[1], 0)
        for e in range(N_EXPERTS):
            lo = start_ref[e] + cnt_ref[e]
            hi = start_ref[e] + pad_ref[e]
            lax.fori_loop(lo, hi, lambda r, c: (_row_copy(zeros, 0, xs_hbm, r, sem_z).wait(), c)[1], 0)

        def tail_copy(j):
            rows = pl.ds(pl.multiple_of(j * (blk * SUBLANES), blk * SUBLANES), blk * SUBLANES)
            return pltpu.make_async_copy(zeros, xs_hbm.at[rows], sem_z)

        n_blocks = xs_hbm.shape[0] // (blk * SUBLANES)
        lax.fori_loop(nused_ref[0], n_blocks, lambda j, c: (tail_copy(j).start(), c)[1], 0)
        lax.fori_loop(nused_ref[0], n_blocks, lambda j, c: (tail_copy(j).wait(), c)[1], 0)

    idx_copy = pltpu.make_async_copy(dest_hbm.at[pl.ds(i * (TOP_K * tm), TOP_K * tm)], idx_smem, sem_i)
    idx_copy.start()
    idx_copy.wait()

    def issue(t, c):
        for k in range(TOP_K):
            _row_copy(h2_ref, t, xs_hbm, idx_smem[k * tm + t], sem_d).start(priority=k % 2)
        return c

    def drain(t, c):
        for k in range(TOP_K):
            _row_copy(h2_ref, 0, xs_hbm, 0, sem_d).wait()
        return c

    lax.fori_loop(0, tm, issue, 0)
    lax.fori_loop(0, tm, drain, 0)


def _dispatch_call(counts, padded, starts, n_used, dest_flat, h2, n_rows, tm):
    n = h2.shape[0] // SUBLANES
    any_spec = pl.BlockSpec(memory_space=pl.ANY)
    return pl.pallas_call(
        functools.partial(_dispatch_kernel, tm=tm, blk=EXPERT_BLK),
        grid_spec=pltpu.PrefetchScalarGridSpec(
            num_scalar_prefetch=4,
            grid=(n // tm,),
            in_specs=[any_spec, pl.BlockSpec((tm * SUBLANES, LANES), lambda i, *_: (i, 0))],
            out_specs=any_spec,
            scratch_shapes=[pltpu.SMEM((TOP_K * tm,), jnp.int32),
                            pltpu.VMEM((EXPERT_BLK * SUBLANES, LANES), h2.dtype),
                            pltpu.SemaphoreType.DMA(()), pltpu.SemaphoreType.DMA(()),
                            pltpu.SemaphoreType.DMA(())]),
        out_shape=jax.ShapeDtypeStruct((n_rows * SUBLANES, LANES), h2.dtype),
        compiler_params=_cparams(1, 24),
    )(counts, padded, starts, n_used, dest_flat, h2)


def _experts_kernel(bexp_ref, nused_ref, xs_ref, win_ref, bin_ref, wout_ref, bout_ref, ys_ref, win_bf, wout_bf):
    i = pl.program_id(0)
    prev = bexp_ref[jnp.maximum(i - 1, 0)]
    new_expert = jnp.logical_or(i == 0, bexp_ref[i] != prev)

    @pl.when(jnp.logical_and(i < nused_ref[0], new_expert))
    def _():
        win_bf[...] = win_ref[0].astype(BF16)
        wout_bf[...] = wout_ref[0].astype(BF16)

    @pl.when(i < nused_ref[0])
    def _():
        dff = wout_bf.shape[0]
        hcat = jnp.dot(_load_rows(xs_ref).astype(BF16), win_bf[...], preferred_element_type=F32) + bin_ref[0]
        x_glu = jnp.minimum(hcat[:, :dff], SWIGLU_LIMIT)
        x_lin = jnp.clip(hcat[:, dff:], -SWIGLU_LIMIT, SWIGLU_LIMIT)
        act = x_glu * (1.0 / (1.0 + jnp.exp(-SWIGLU_ALPHA * x_glu))) * (x_lin + 1.0)
        _store_rows(ys_ref, jnp.dot(act.astype(BF16), wout_bf[...], preferred_element_type=F32) + bout_ref[0])

    @pl.when(i >= nused_ref[0])
    def _():
        ys_ref[...] = jnp.zeros_like(ys_ref)


def _experts_call(block_exp, n_used, xs, w_in, b_in, w_out, b_out):
    n_rows = xs.shape[0] // SUBLANES
    ne, dm, dff2 = w_in.shape
    dff = dff2 // 2
    blk = EXPERT_BLK
    rows = pl.BlockSpec((blk * SUBLANES, LANES), lambda i, be, nu: (jnp.minimum(i, nu[0] - 1), 0))
    return pl.pallas_call(
        _experts_kernel,
        grid_spec=pltpu.PrefetchScalarGridSpec(
            num_scalar_prefetch=2,
            grid=(n_rows // blk,),
            in_specs=[rows,
                      pl.BlockSpec((1, dm, dff2), lambda i, be, nu: (be[i], 0, 0)),
                      pl.BlockSpec((1, 1, dff2), lambda i, be, nu: (be[i], 0, 0)),
                      pl.BlockSpec((1, dff, dm), lambda i, be, nu: (be[i], 0, 0)),
                      pl.BlockSpec((1, 1, dm), lambda i, be, nu: (be[i], 0, 0))],
            out_specs=pl.BlockSpec((blk * SUBLANES, LANES), lambda i, be, nu: (i, 0)),
            scratch_shapes=[pltpu.VMEM((dm, dff2), BF16), pltpu.VMEM((dff, dm), BF16)]),
        out_shape=jax.ShapeDtypeStruct((n_rows * SUBLANES, LANES), F32),
        compiler_params=_cparams(1, 56),
    )(block_exp, n_used, xs, w_in, b_in.reshape(ne, 1, dff2), w_out, b_out.reshape(ne, 1, dm))


def _combine_kernel(dest_hbm, ys_hbm, x1_ref, gate_ref, gt_ref, gf_ref, o_ref, idx_smem, rows, sem_i, sem_d, *, tm):
    i = pl.program_id(0)

    def fetch(step, slot):
        idx_copy = pltpu.make_async_copy(dest_hbm.at[pl.ds(step * (TOP_K * tm), TOP_K * tm)], idx_smem, sem_i)
        idx_copy.start()
        idx_copy.wait()

        def issue(t, c):
            for k in range(TOP_K):
                _row_copy(ys_hbm, idx_smem[k * tm + t], rows.at[slot, k], t, sem_d.at[slot]).start(priority=k % 2)
            return c

        lax.fori_loop(0, tm, issue, 0)

    @pl.when(i == 0)
    def _():
        fetch(0, 0)

    for parity in range(2):
        @pl.when(jnp.logical_and(i + 1 < pl.num_programs(0), (i + 1) % 2 == parity))
        def _():
            fetch(i + 1, parity)

    slot = i % 2

    def drain(t, c):
        for k in range(TOP_K):
            _row_copy(ys_hbm, 0, rows.at[slot, k], 0, sem_d.at[slot]).wait()
        return c

    lax.fori_loop(0, tm, drain, 0)

    gates = gate_ref[...]
    y = gates[:, 0:1] * _load_rows(rows.at[slot, 0])
    for k in range(1, TOP_K):
        y = y + gates[:, k:k + 1] * _load_rows(rows.at[slot, k])
    x2 = x1_ref[...] + gt_ref[0] * y
    ms = jnp.mean(x2 * x2, axis=-1, keepdims=True)
    o_ref[...] = x2 * lax.rsqrt(ms + EPS) * gf_ref[...]


def _combine_call(dest_flat, ys, x1, gate_t, gt2, g_final, tiles_per_batch, tm):
    n, dm = x1.shape
    row = lambda w: pl.BlockSpec((tm, w), lambda i: (i, 0))
    return pl.pallas_call(
        functools.partial(_combine_kernel, tm=tm),
        grid=(n // tm,),
        in_specs=[pl.BlockSpec(memory_space=pl.ANY), pl.BlockSpec(memory_space=pl.ANY),
                  row(dm), row(LANES),
                  pl.BlockSpec((1, 1, dm), lambda i: (i // tiles_per_batch, 0, 0)),
                  pl.BlockSpec((1, dm), lambda i: (0, 0))],
        out_specs=row(dm),
        out_shape=jax.ShapeDtypeStruct((n, dm), F32),
        scratch_shapes=[pltpu.SMEM((TOP_K * tm,), jnp.int32),
                        pltpu.VMEM((2, TOP_K, tm * SUBLANES, LANES), F32),
                        pltpu.SemaphoreType.DMA(()), pltpu.SemaphoreType.DMA((2,))],
        compiler_params=_cparams(1, 40),
    )(dest_flat, ys, x1, gate_t, gt2, g_final)


def _rope_tables(t):
    rows = t // GRID_W
    n_freq = HEAD_DIM // 4
    inv_freq = ROPE_THETA ** (-jnp.arange(n_freq, dtype=F32) / n_freq)
    ar = jnp.arange(rows, dtype=F32)[:, None] * inv_freq
    ac = jnp.arange(GRID_W, dtype=F32)[:, None] * inv_freq
    per_row = lambda a: jnp.broadcast_to(a[:, None, :], (rows, GRID_W, n_freq))
    per_col = lambda a: jnp.broadcast_to(a[None, :, :], (rows, GRID_W, n_freq))
    cr, sr, cc, sc = per_row(jnp.cos(ar)), per_row(jnp.sin(ar)), per_col(jnp.cos(ac)), per_col(jnp.sin(ac))
    cos = jnp.concatenate([cr, cr, cc, cc] * 2, axis=-1).reshape(t, LANES)
    sin = jnp.concatenate([-sr, sr, -sc, sc] * 2, axis=-1).reshape(t, LANES)
    return cos, sin


def kernel(x, c, ctx, c_ctx, w_ada, b_ada, g_attn, w_qkv, gqa_q_norm, gqa_k_norm, diff_lambda,
           diff_subln, w_o, g_ffn, w_router, b_router, w_in, b_in, w_out, b_out, g_final):
    assert w_ada.shape[0] == 1, "single-layer block"
    b, t, dm = x.shape
    n = b * t

    cv = jnp.zeros((8, dm), F32).at[:b].set(c).at[b].set(c_ctx)
    mod = _mod_call(cv, w_ada[0], b_ada[0])
    sh1, sc1, gt1, sh2, sc2, gt2 = [m[:b, None, :] for m in jnp.split(mod, N_MOD, axis=-1)]
    csh1, csc1 = [jnp.broadcast_to(m[b][None, None, :], (b, 1, dm)) for m in jnp.split(mod, N_MOD, axis=-1)[:2]]

    wq = w_qkv[0]
    o_ka, o_va, o_qb, o_kb, o_vb = 512, 640, 768, 1280, 1792
    w_kv = jnp.concatenate([wq[:, o_ka:o_qb], wq[:, o_kb:]], axis=1)
    w_all = jnp.concatenate([w_kv, wq[:, :o_ka], wq[:, o_qb:o_kb]], axis=1).astype(BF16)
    gq = jnp.tile(gqa_q_norm[0], 2)[None, :]
    gk = jnp.tile(gqa_k_norm[0], 2)[None, :]
    lane = jnp.arange(LANES)
    bd = (lane[:, None] // HEAD_DIM == lane[None, :] // HEAD_DIM).astype(BF16)
    cos, sin = _rope_tables(t)
    g_a = g_attn[0][None, :]

    ka, va, kb, vb, qa, qb = _proj_call(x, sh1, sc1, g_a, w_all, cos, sin, gq, gk, bd, rope=True, with_q=True)
    ka_c, va_c, kb_c, vb_c = _proj_call(ctx, csh1, csc1, g_a, w_all[:, :KV_COLS], cos, sin, gq, gk, bd,
                                        rope=False, with_q=False)

    keys = lambda ctx_part, lat_part: jnp.concatenate([ctx_part, lat_part], axis=2)
    att_a = _gqa_call(qa, keys(ka_c, ka), keys(va_c, va))
    att_d = _diff_call(qb, keys(kb_c, kb), keys(vb_c, vb), diff_lambda[0], diff_subln[0][None, :])

    wr_f32 = jnp.zeros((dm, LANES), F32).at[:, :N_EXPERTS].set(w_router[0])
    wr_hi = wr_f32.astype(BF16)
    wr_pad = jnp.stack([wr_hi, (wr_f32 - wr_hi.astype(F32)).astype(BF16)])
    br_pad = jnp.zeros((1, LANES), F32).at[0, :N_EXPERTS].set(b_router[0])
    x1, h2, idx, rank, gate_t, cnt = _oproj_call(
        att_a.reshape(n, -1), att_d.reshape(n, -1), x.reshape(n, dm), gt1, sh2, sc2, g_ffn[0][None, :],
        w_o[0].astype(BF16), wr_pad, br_pad, t // OPROJ_TM)

    blk = EXPERT_BLK
    counts = cnt[:, 0].astype(jnp.int32)
    padded = (counts + blk - 1) // blk * blk
    pad_ends = jnp.cumsum(padded)
    starts = pad_ends - padded
    n_blocks = n * TOP_K // blk + N_EXPERTS
    expert_ids = jnp.arange(N_EXPERTS, dtype=jnp.int32)
    dest = rank + jnp.sum(jnp.where(idx[..., None] == expert_ids, starts, 0), axis=-1)
    tiled = lambda tm: dest.reshape(TOP_K, n // tm, tm).transpose(1, 0, 2).reshape(-1)
    block_row0 = jnp.arange(n_blocks, dtype=jnp.int32) * blk
    block_exp = jnp.minimum(jnp.sum((pad_ends[None, :] <= block_row0[:, None]).astype(jnp.int32), axis=1),
                            N_EXPERTS - 1)
    n_used = (pad_ends[-1:] // blk).astype(jnp.int32)

    dtm = min(DISPATCH_TM, t)
    ctm = min(COMBINE_TM, t)
    xs = _dispatch_call(counts, padded, starts, n_used, tiled(dtm), h2, n_blocks * blk, dtm)
    ys = _experts_call(block_exp, n_used, xs, w_in[0], b_in[0], w_out[0], b_out[0])
    out = _combine_call(tiled(ctm), ys, x1, gate_t, gt2, g_final[None, :], t // ctm, ctm)
    return out.reshape(b, t, dm)
```

```python
import functools

import jax
import jax.numpy as jnp
from jax import lax
from jax.experimental import pallas as pl
from jax.experimental.pallas import tpu as pltpu

F32 = jnp.float32
BF16 = jnp.bfloat16

HEAD_DIM = 64
GQA_HEADS = 8
GQA_KV_HEADS = 2
GQA_GROUP = GQA_HEADS // GQA_KV_HEADS
DIFF_HEADS = 4
DIFF_V_DIM = 2 * HEAD_DIM
GRID_W = 64
ROPE_THETA = 10000.0
N_EXPERTS = 32
TOP_K = 4
SWIGLU_ALPHA = 1.702
SWIGLU_LIMIT = 7.0
N_MOD = 6
EPS = 1e-6
LAMBDA_INIT = 0.8 - 0.6
LOG2E = 1.4426950408889634
Q_SCALE = HEAD_DIM ** -0.5 * LOG2E

LANES = 128
SUBLANES = 8
GQA_V_EXT = LANES
KV_COLS = 2 * GQA_KV_HEADS * HEAD_DIM + DIFF_HEADS * 2 * HEAD_DIM + DIFF_HEADS * DIFF_V_DIM
Q_COLS = GQA_HEADS * HEAD_DIM + DIFF_HEADS * 2 * HEAD_DIM

PROJ_TM = 512
GQA_TQ = 256
DIFF_TQ = 512
ATT_TK = 2816
MXU_TILE = 256
OPROJ_TM = 512
INVERT_CHUNK = 1024
TOKEN_BITS = 15
TOKEN_MASK = (1 << TOKEN_BITS) - 1
COMBINE_TM = 512
EXPERT_BLK = 512


def _store_rows(ref, value):
    m = value.shape[0]
    for s in range(SUBLANES):
        ref[pl.ds(s, m, stride=SUBLANES), :] = value[:, s * LANES:(s + 1) * LANES]


def _load_rows(ref):
    m = ref.shape[0] // SUBLANES
    return jnp.concatenate([ref[pl.ds(s, m, stride=SUBLANES), :] for s in range(SUBLANES)], axis=1)


def _cparams(n_axes, vmem_mb):
    return pltpu.CompilerParams(dimension_semantics=("arbitrary",) * n_axes,
                                vmem_limit_bytes=vmem_mb * 1024 * 1024)


def _mod_kernel(cv_ref, w_ref, b_ref, o_ref):
    cv = cv_ref[...]
    s = cv * (1.0 / (1.0 + jnp.exp(-cv)))
    o_ref[...] = jnp.dot(s, w_ref[...], preferred_element_type=F32,
                         precision=lax.Precision.HIGHEST) + b_ref[...]


def _mod_call(cv, w_ada, b_ada):
    d, n = w_ada.shape
    tn = 1024
    return pl.pallas_call(
        _mod_kernel,
        grid=(n // tn,),
        in_specs=[pl.BlockSpec((8, d), lambda j: (0, 0)),
                  pl.BlockSpec((d, tn), lambda j: (0, j)),
                  pl.BlockSpec((1, tn), lambda j: (0, j))],
        out_specs=pl.BlockSpec((8, tn), lambda j: (0, j)),
        out_shape=jax.ShapeDtypeStruct((8, n), F32),
        compiler_params=_cparams(1, 32),
    )(cv, w_ada, b_ada.reshape(1, n))


def _head_rms(v, gain, bd):
    sq = v * v
    hi = sq.astype(BF16)
    lo = (sq - hi.astype(F32)).astype(BF16)
    ss = (jnp.dot(hi, bd, preferred_element_type=F32) + jnp.dot(lo, bd, preferred_element_type=F32))
    return v * lax.rsqrt(ss * (1.0 / HEAD_DIM) + EPS) * gain


def _rope_chunk(v, cos, sin, first_half):
    partner = jnp.where(first_half, pltpu.roll(v, LANES - 16, 1), pltpu.roll(v, 16, 1))
    return v * cos + partner * sin


def _proj_kernel(x_ref, sh_ref, sc_ref, g_ref, w_ref, cos_ref, sin_ref, gq_ref, gk_ref, bd_ref,
                 *out_refs, rope, with_q):
    xf = x_ref[0]
    ms = jnp.mean(xf * xf, axis=-1, keepdims=True)
    h = xf * lax.rsqrt(ms + EPS) * g_ref[...]
    h = h * (1.0 + sc_ref[0]) + sh_ref[0]
    p = jnp.dot(h.astype(BF16), w_ref[...], preferred_element_type=F32)

    if with_q:
        ka_ref, va_ref, kb_ref, vb_ref, qa_ref, qb_ref = out_refs
    else:
        ka_ref, va_ref, kb_ref, vb_ref = out_refs
    bd = bd_ref[...]
    if rope:
        cos = cos_ref[...]
        sin = sin_ref[...]
        lane = lax.broadcasted_iota(jnp.int32, cos.shape, 1)
        first_half = (lane % 32) < 16

    def chunk(j):
        return p[:, j * LANES:(j + 1) * LANES]

    def maybe_rope(v):
        return _rope_chunk(v, cos, sin, first_half) if rope else v

    def put_heads(ref, j, v):
        ref[0, 2 * j] = v[:, :HEAD_DIM].astype(ref.dtype)
        ref[0, 2 * j + 1] = v[:, HEAD_DIM:].astype(ref.dtype)

    put_heads(ka_ref, 0, maybe_rope(_head_rms(chunk(0), gk_ref[...], bd)))
    lanes = lax.broadcasted_iota(jnp.int32, (p.shape[0], LANES), 1)
    va = chunk(1)
    for h, v in enumerate((va, pltpu.roll(va, HEAD_DIM, 1))):
        va_ref[0, h] = jnp.where(lanes < HEAD_DIM, v, (lanes == HEAD_DIM).astype(F32)).astype(va_ref.dtype)
    for j in range(4):
        put_heads(kb_ref, j, maybe_rope(chunk(2 + j)))
        vb_ref[0, j] = chunk(6 + j).astype(vb_ref.dtype)
    if with_q:
        for j in range(4):
            put_heads(qa_ref, j, maybe_rope(_head_rms(chunk(10 + j), gq_ref[...], bd)) * Q_SCALE)
            put_heads(qb_ref, j, maybe_rope(chunk(14 + j)) * Q_SCALE)


def _proj_call(x, shift, scale, g, w, cos, sin, gq, gk, bd, *, rope, with_q):
    nb, t, d = x.shape
    tm = min(PROJ_TM, t)
    ncols = w.shape[1]
    heads = lambda n, w: pl.BlockSpec((1, n, tm, w), lambda b, i: (b, 0, i, 0))
    head = lambda n: heads(n, HEAD_DIM)
    out_specs = [head(2), heads(2, GQA_V_EXT), head(8), heads(4, DIFF_V_DIM)]
    out_shape = [jax.ShapeDtypeStruct((nb, 2, t, HEAD_DIM), BF16),
                 jax.ShapeDtypeStruct((nb, 2, t, GQA_V_EXT), BF16),
                 jax.ShapeDtypeStruct((nb, 8, t, HEAD_DIM), BF16),
                 jax.ShapeDtypeStruct((nb, 4, t, DIFF_V_DIM), BF16)]
    if with_q:
        out_specs += [head(8), head(8)]
        out_shape += [jax.ShapeDtypeStruct((nb, 8, t, HEAD_DIM), BF16)] * 2
    const = lambda shape: pl.BlockSpec(shape, lambda b, i: (0,) * len(shape))
    return pl.pallas_call(
        functools.partial(_proj_kernel, rope=rope, with_q=with_q),
        grid=(nb, t // tm),
        in_specs=[pl.BlockSpec((1, tm, d), lambda b, i: (b, i, 0)),
                  pl.BlockSpec((1, 1, d), lambda b, i: (b, 0, 0)),
                  pl.BlockSpec((1, 1, d), lambda b, i: (b, 0, 0)),
                  const((1, d)),
                  const((d, ncols)),
                  pl.BlockSpec((tm, LANES), lambda b, i: (i, 0)),
                  pl.BlockSpec((tm, LANES), lambda b, i: (i, 0)),
                  const((1, LANES)), const((1, LANES)), const((LANES, LANES))],
        out_specs=out_specs,
        out_shape=out_shape,
        compiler_params=_cparams(2, 48),
    )(x, shift, scale, g, w, cos, sin, gq, gk, bd)


def _scores(q, k):
    return lax.dot_general(q, k, (((1,), (1,)), ((), ())), preferred_element_type=F32)


def _softmax_step(s, v, carry, sum_on_mxu):
    m, l, acc = carry
    m_new = jnp.maximum(m, jnp.max(s, axis=1, keepdims=True))
    alpha = jnp.exp2(m - m_new)
    p = jnp.exp2(s - m_new)
    if not sum_on_mxu:
        l = alpha * l + jnp.sum(p, axis=1, keepdims=True)
    acc_new = alpha * acc + jnp.dot(p.astype(BF16), v, preferred_element_type=F32)
    return m_new, l, acc_new


def _init_carry(m_rows, dv):
    return (jnp.full((m_rows, 1), -jnp.inf, F32), jnp.zeros((m_rows, 1), F32), jnp.zeros((m_rows, dv), F32))


def _key_tile(n_keys):
    return max(tk for tk in range(MXU_TILE, min(ATT_TK, n_keys) + 1, MXU_TILE) if n_keys % tk == 0)


def _attend(scores, k_count, v_ref, m_rows, dv, tk):
    sum_on_mxu = v_ref.shape[3] > dv
    carry = _init_carry(m_rows, v_ref.shape[3])
    for j in range(k_count // tk):
        rows = pl.ds(j * tk, tk)
        carry = _softmax_step(scores(rows), v_ref[0, 0, rows, :], carry, sum_on_mxu)
    _, l, acc = carry
    return acc[:, :dv] / (acc[:, dv:dv + 1] if sum_on_mxu else l)


def _gqa_kernel(q_ref, k_ref, v_ref, o_ref, *, tk):
    tq = q_ref.shape[2]
    q = q_ref[0].reshape(GQA_GROUP * tq, HEAD_DIM)
    o = _attend(lambda rows: _scores(q, k_ref[0, 0, rows, :]), k_ref.shape[2], v_ref,
                GQA_GROUP * tq, HEAD_DIM, tk).reshape(GQA_GROUP, tq, HEAD_DIM)
    for g in range(GQA_GROUP):
        o_ref[0, :, g * HEAD_DIM:(g + 1) * HEAD_DIM] = o[g].astype(o_ref.dtype)


def _gqa_call(qa, ka, va):
    b, _, t, _ = qa.shape
    nk = ka.shape[2]
    tq = min(GQA_TQ, t)
    kv = lambda w: pl.BlockSpec((1, 1, nk, w), lambda bi, h, i: (bi, h, 0, 0))
    return pl.pallas_call(
        functools.partial(_gqa_kernel, tk=_key_tile(nk)),
        grid=(b, GQA_KV_HEADS, t // tq),
        in_specs=[pl.BlockSpec((1, GQA_GROUP, tq, HEAD_DIM), lambda bi, h, i: (bi, h, i, 0)),
                  kv(HEAD_DIM), kv(va.shape[3])],
        out_specs=pl.BlockSpec((1, tq, GQA_GROUP * HEAD_DIM), lambda bi, h, i: (bi, i, h)),
        out_shape=jax.ShapeDtypeStruct((b, t, GQA_HEADS * HEAD_DIM), BF16),
        compiler_params=_cparams(3, 48),
    )(qa, ka, va)


def _diff_kernel(q_ref, k_ref, v_ref, lam_ref, sg_ref, o_ref, *, tk):
    tq = q_ref.shape[2]
    q = (q_ref[0, 0], q_ref[0, 1])

    def scores(rows):
        return jnp.concatenate([_scores(q[c], k_ref[0, c, rows, :]) for c in range(2)], axis=0)

    o_both = _attend(scores, k_ref.shape[2], v_ref, 2 * tq, DIFF_V_DIM, tk)
    lamf = lam_ref[...]
    lam = (jnp.exp(jnp.sum(lamf[0:1] * lamf[1:2], axis=-1, keepdims=True))
           - jnp.exp(jnp.sum(lamf[2:3] * lamf[3:4], axis=-1, keepdims=True)) + LAMBDA_INIT)
    o = o_both[:tq] - lam * o_both[tq:]
    ms = jnp.mean(o * o, axis=-1, keepdims=True)
    o = o * lax.rsqrt(ms + EPS) * sg_ref[...] * (1.0 - LAMBDA_INIT)
    o_ref[0] = o.astype(o_ref.dtype)


def _diff_call(qb, kb, vb, lam, sub_g):
    b, _, t, _ = qb.shape
    nk = kb.shape[2]
    tq = min(DIFF_TQ, t)
    return pl.pallas_call(
        functools.partial(_diff_kernel, tk=_key_tile(nk)),
        grid=(b, DIFF_HEADS, t // tq),
        in_specs=[pl.BlockSpec((1, 2, tq, HEAD_DIM), lambda bi, h, i: (bi, h, i, 0)),
                  pl.BlockSpec((1, 2, nk, HEAD_DIM), lambda bi, h, i: (bi, h, 0, 0)),
                  pl.BlockSpec((1, 1, nk, vb.shape[3]), lambda bi, h, i: (bi, h, 0, 0)),
                  pl.BlockSpec((4, HEAD_DIM), lambda bi, h, i: (0, 0)),
                  pl.BlockSpec((1, DIFF_V_DIM), lambda bi, h, i: (0, 0))],
        out_specs=pl.BlockSpec((1, tq, DIFF_V_DIM), lambda bi, h, i: (bi, i, h)),
        out_shape=jax.ShapeDtypeStruct((b, t, DIFF_HEADS * DIFF_V_DIM), BF16),
        compiler_params=_cparams(3, 48),
    )(qb, kb, vb, lam, sub_g)


def _oproj_kernel(a_ref, d_ref, x_ref, gt_ref, sh_ref, sc_ref, g_ref, wo_ref, wr_ref, br_ref, tri_ref,
                  x1_ref, h2_ref, idx_ref, rank_ref, gate_ref, cnt_ref):
    i = pl.program_id(0)

    @pl.when(i == 0)
    def _():
        cnt_ref[...] = jnp.zeros_like(cnt_ref)

    half = a_ref.shape[1]
    mix = (jnp.dot(a_ref[...], wo_ref[0:half, :], preferred_element_type=F32)
           + jnp.dot(d_ref[...], wo_ref[half:2 * half, :], preferred_element_type=F32))
    x1 = x_ref[...] + gt_ref[0] * mix
    x1_ref[...] = x1
    ms = jnp.mean(x1 * x1, axis=-1, keepdims=True)
    h2 = x1 * lax.rsqrt(ms + EPS) * g_ref[...]
    h2 = h2 * (1.0 + sc_ref[0]) + sh_ref[0]
    _store_rows(h2_ref, h2)

    h_hi = h2.astype(BF16)
    h_lo = (h2 - h_hi.astype(F32)).astype(BF16)
    logits = (jnp.dot(h_hi, wr_ref[0], preferred_element_type=F32)
              + jnp.dot(h_lo, wr_ref[0], preferred_element_type=F32)
              + jnp.dot(h_hi, wr_ref[1], preferred_element_type=F32)) + br_ref[...]
    lt = logits.T[0:N_EXPERTS, :]
    tm = lt.shape[1]
    eid = lax.broadcasted_iota(jnp.int32, lt.shape, 0).astype(F32)
    vals, sels = [], []
    for k in range(TOP_K):
        m = jnp.max(lt, axis=0, keepdims=True)
        first = jnp.min(jnp.where(lt == m, eid, float(N_EXPERTS)), axis=0, keepdims=True)
        sel = eid == first
        lt = jnp.where(sel, -jnp.inf, lt)
        vals.append(m)
        sels.append(sel)
        idx_ref[k:k + 1, :] = first.astype(jnp.int32)

    onehot = sum(s.astype(F32) for s in sels)
    before = cnt_ref[:, 0:1] + jnp.dot(onehot.astype(BF16), tri_ref[...], preferred_element_type=F32)
    for k in range(TOP_K):
        rank_ref[k:k + 1, :] = jnp.sum(jnp.where(sels[k], before, 0.0), axis=0, keepdims=True).astype(jnp.int32)
    cnt_ref[...] = cnt_ref[...] + jnp.sum(onehot, axis=1, keepdims=True)

    es = [jnp.exp(v - vals[0]) for v in vals]
    den = es[0] + es[1] + es[2] + es[3]
    gates = jnp.concatenate([e / den for e in es] + [jnp.zeros((LANES - TOP_K, tm), F32)], axis=0)
    gate_ref[...] = gates.T


def _oproj_call(a, d, x, gt1, sh2, sc2, g_ffn, w_o, wr_pad, br_pad, tiles_per_batch):
    n, dm = x.shape
    tm = OPROJ_TM
    tri = (lax.broadcasted_iota(jnp.int32, (tm, tm), 0) < lax.broadcasted_iota(jnp.int32, (tm, tm), 1)).astype(BF16)
    row = lambda w: pl.BlockSpec((tm, w), lambda i: (i, 0))
    per_batch = pl.BlockSpec((1, 1, dm), lambda i: (i // tiles_per_batch, 0, 0))
    const = lambda shape: pl.BlockSpec(shape, lambda i: (0,) * len(shape))
    return pl.pallas_call(
        _oproj_kernel,
        grid=(n // tm,),
        in_specs=[row(a.shape[1]), row(d.shape[1]), row(dm), per_batch, per_batch, per_batch,
                  const((1, dm)), const(w_o.shape), const(wr_pad.shape), const((1, LANES)), const((tm, tm))],
        out_specs=[row(dm), pl.BlockSpec((tm * SUBLANES, LANES), lambda i: (i, 0)),
                   pl.BlockSpec((TOP_K, tm), lambda i: (0, i)),
                   pl.BlockSpec((TOP_K, tm), lambda i: (0, i)),
                   row(LANES),
                   const((N_EXPERTS, LANES))],
        out_shape=[jax.ShapeDtypeStruct((n, dm), F32),
                   jax.ShapeDtypeStruct((n * SUBLANES, LANES), F32),
                   jax.ShapeDtypeStruct((TOP_K, n), jnp.int32),
                   jax.ShapeDtypeStruct((TOP_K, n), jnp.int32),
                   jax.ShapeDtypeStruct((n, LANES), F32),
                   jax.ShapeDtypeStruct((N_EXPERTS, LANES), F32)],
        compiler_params=_cparams(1, 48),
    )(a, d, x, gt1, sh2, sc2, g_ffn, w_o, wr_pad, br_pad, tri)


def _row_copy(src_ref, src_row, dst_ref, dst_row, sem):
    tile = lambda r: pl.ds(pl.multiple_of(r * SUBLANES, SUBLANES), SUBLANES)
    return pltpu.make_async_copy(src_ref.at[tile(src_row)], dst_ref.at[tile(dst_row)], sem)


def _experts_kernel(bexp_ref, nused_ref, cnt_ref, pad_ref, start_ref, dest_hbm, h2_hbm, win_ref, bin_ref, wout_ref,
                    bout_ref, y_hbm, win_bf, wout_bf, row_map, dest_buf, landing, x_bf, staged, sem_i, sem_g, sem_s,
                    *, chunk, n_tok):
    i = pl.program_id(0)
    blk = EXPERT_BLK
    n_used = nused_ref[0]

    def gather_start(block, r):
        _row_copy(h2_hbm, row_map[block * blk + r] & TOKEN_MASK, landing, r, sem_g).start()

    def gather_wait():
        _row_copy(h2_hbm, 0, landing, 0, sem_g).wait()

    def scatter_start(block, r):
        _row_copy(staged, r, y_hbm, lax.shift_right_logical(row_map[block * blk + r], TOKEN_BITS), sem_s).start()

    def scatter_wait():
        _row_copy(staged, 0, y_hbm, 0, sem_s).wait()

    @pl.when(i == 0)
    def _():
        staged[...] = jnp.zeros_like(staged)
        spare = pltpu.make_async_copy(staged, y_hbm.at[pl.ds(TOP_K * n_tok * SUBLANES, blk * SUBLANES)], sem_s)
        spare.start()
        spare.wait()
        for e in range(N_EXPERTS):
            lo = start_ref[e] + cnt_ref[e]
            hi = start_ref[e] + pad_ref[e]

            def mark_padding(r, c):
                row_map[r] = (TOP_K * n_tok + lax.rem(r, blk)) << TOKEN_BITS
                return c

            lax.fori_loop(lo, hi, mark_padding, 0)

        def invert_chunk(c, carry):
            copy = pltpu.make_async_copy(dest_hbm.at[pl.ds(c * (TOP_K * chunk), TOP_K * chunk)], dest_buf, sem_i)
            copy.start()
            copy.wait()

            def per_token(t, c2):
                for k in range(TOP_K):
                    tok = c * chunk + t
                    row_map[dest_buf[k * chunk + t]] = ((k * n_tok + tok) << TOKEN_BITS) | tok
                return c2

            return lax.fori_loop(0, chunk, per_token, carry)

        lax.fori_loop(0, dest_hbm.shape[0] // (TOP_K * chunk), invert_chunk, 0)

        def first(r, c):
            gather_start(0, r)
            return c

        lax.fori_loop(0, blk, first, 0)

    prev = bexp_ref[jnp.maximum(i - 1, 0)]
    new_expert = jnp.logical_or(i == 0, bexp_ref[i] != prev)

    @pl.when(jnp.logical_and(i < n_used, new_expert))
    def _():
        win_bf[...] = win_ref[0].astype(BF16)
        wout_bf[...] = wout_ref[0].astype(BF16)

    @pl.when(i < n_used)
    def _():
        dff = wout_bf.shape[0]
        for r in range(blk):
            gather_wait()
        x_bf[...] = _load_rows(landing).astype(BF16)
        nxt = jnp.minimum(i + 1, n_used - 1)
        done = jnp.maximum(i - 1, 0)
        for r in range(blk):
            gather_start(nxt, r)
        for r in range(blk):
            scatter_start(done, r)
        hcat = jnp.dot(x_bf[...], win_bf[...], preferred_element_type=F32) + bin_ref[0]
        x_glu = jnp.minimum(hcat[:, :dff], SWIGLU_LIMIT)
        x_lin = jnp.clip(hcat[:, dff:], -SWIGLU_LIMIT, SWIGLU_LIMIT)
        act = x_glu * (1.0 / (1.0 + jnp.exp(-SWIGLU_ALPHA * x_glu))) * (x_lin + 1.0)
        y = jnp.dot(act.astype(BF16), wout_bf[...], preferred_element_type=F32) + bout_ref[0]
        for r in range(blk):
            scatter_wait()
        _store_rows(staged, y)

    @pl.when(i == n_used - 1)
    def _():
        for r in range(blk):
            gather_wait()

    @pl.when(i == n_used)
    def _():
        def last(r, c):
            scatter_start(n_used - 1, r)
            return c

        lax.fori_loop(0, blk, last, 0)
        for r in range(blk):
            scatter_wait()


def _experts_call(block_exp, n_used, counts, padded, starts, dest_flat, h2, w_in, b_in, w_out, b_out, n_rows, chunk):
    ne, dm, dff2 = w_in.shape
    dff = dff2 // 2
    blk = EXPERT_BLK
    n_tok = h2.shape[0] // SUBLANES
    assert n_rows // blk > (n_tok * TOP_K) // blk, "needs a grid step after the last used block"
    assert n_tok <= TOKEN_MASK and (TOP_K * n_tok + blk) <= (1 << (32 - TOKEN_BITS)), "row_map word layout"
    any_spec = pl.BlockSpec(memory_space=pl.ANY)
    by_expert = lambda shape: pl.BlockSpec((1,) + shape, lambda i, be, *_: (be[i], 0, 0))
    return pl.pallas_call(
        functools.partial(_experts_kernel, chunk=chunk, n_tok=n_tok),
        grid_spec=pltpu.PrefetchScalarGridSpec(
            num_scalar_prefetch=5,
            grid=(n_rows // blk,),
            in_specs=[any_spec, any_spec,
                      by_expert((dm, dff2)), by_expert((1, dff2)), by_expert((dff, dm)), by_expert((1, dm))],
            out_specs=any_spec,
            scratch_shapes=[pltpu.VMEM((dm, dff2), BF16), pltpu.VMEM((dff, dm), BF16),
                            pltpu.SMEM((n_rows,), jnp.int32), pltpu.SMEM((TOP_K * chunk,), jnp.int32),
                            pltpu.VMEM((blk * SUBLANES, LANES), F32), pltpu.VMEM((blk, dm), BF16),
                            pltpu.VMEM((blk * SUBLANES, LANES), F32),
                            pltpu.SemaphoreType.DMA(()), pltpu.SemaphoreType.DMA(()), pltpu.SemaphoreType.DMA(())]),
        out_shape=jax.ShapeDtypeStruct(((TOP_K * n_tok + blk) * SUBLANES, LANES), F32),
        compiler_params=_cparams(1, 56),
    )(block_exp, n_used, counts, padded, starts, dest_flat, h2,
      w_in, b_in.reshape(ne, 1, dff2), w_out, b_out.reshape(ne, 1, dm))


def _combine_kernel(*refs):
    y_refs, (x1_ref, gate_ref, gt_ref, gf_ref, o_ref) = refs[:TOP_K], refs[TOP_K:]
    gates = gate_ref[...]
    y = gates[:, 0:1] * _load_rows(y_refs[0])
    for k in range(1, TOP_K):
        y = y + gates[:, k:k + 1] * _load_rows(y_refs[k])
    x2 = x1_ref[...] + gt_ref[0] * y
    ms = jnp.mean(x2 * x2, axis=-1, keepdims=True)
    o_ref[...] = x2 * lax.rsqrt(ms + EPS) * gf_ref[...]


def _combine_call(y, x1, gate_t, gt2, g_final, tiles_per_batch, tm):
    n, dm = x1.shape
    tiles = n // tm
    row = lambda w: pl.BlockSpec((tm, w), lambda i: (i, 0))
    expert_out = lambda k: pl.BlockSpec((tm * SUBLANES, LANES), lambda i: (k * tiles + i, 0))
    return pl.pallas_call(
        _combine_kernel,
        grid=(tiles,),
        in_specs=[expert_out(k) for k in range(TOP_K)] + [
            row(dm), row(LANES),
            pl.BlockSpec((1, 1, dm), lambda i: (i // tiles_per_batch, 0, 0)),
            pl.BlockSpec((1, dm), lambda i: (0, 0))],
        out_specs=row(dm),
        out_shape=jax.ShapeDtypeStruct((n, dm), F32),
        compiler_params=_cparams(1, 40),
    )(y, y, y, y, x1, gate_t, gt2, g_final)


def _rope_tables(t):
    rows = t // GRID_W
    n_freq = HEAD_DIM // 4
    inv_freq = ROPE_THETA ** (-jnp.arange(n_freq, dtype=F32) / n_freq)
    ar = jnp.arange(rows, dtype=F32)[:, None] * inv_freq
    ac = jnp.arange(GRID_W, dtype=F32)[:, None] * inv_freq
    per_row = lambda a: jnp.broadcast_to(a[:, None, :], (rows, GRID_W, n_freq))
    per_col = lambda a: jnp.broadcast_to(a[None, :, :], (rows, GRID_W, n_freq))
    cr, sr, cc, sc = per_row(jnp.cos(ar)), per_row(jnp.sin(ar)), per_col(jnp.cos(ac)), per_col(jnp.sin(ac))
    cos = jnp.concatenate([cr, cr, cc, cc] * 2, axis=-1).reshape(t, LANES)
    sin = jnp.concatenate([-sr, sr, -sc, sc] * 2, axis=-1).reshape(t, LANES)
    return cos, sin


def kernel(x, c, ctx, c_ctx, w_ada, b_ada, g_attn, w_qkv, gqa_q_norm, gqa_k_norm, diff_lambda,
           diff_subln, w_o, g_ffn, w_router, b_router, w_in, b_in, w_out, b_out, g_final):
    assert w_ada.shape[0] == 1, "single-layer block"
    b, t, dm = x.shape
    n = b * t

    cv = jnp.zeros((8, dm), F32).at[:b].set(c).at[b].set(c_ctx)
    mod = _mod_call(cv, w_ada[0], b_ada[0])
    sh1, sc1, gt1, sh2, sc2, gt2 = [m[:b, None, :] for m in jnp.split(mod, N_MOD, axis=-1)]
    csh1, csc1 = [jnp.broadcast_to(m[b][None, None, :], (b, 1, dm)) for m in jnp.split(mod, N_MOD, axis=-1)[:2]]

    wq = w_qkv[0]
    o_ka, o_va, o_qb, o_kb, o_vb = 512, 640, 768, 1280, 1792
    w_kv = jnp.concatenate([wq[:, o_ka:o_qb], wq[:, o_kb:]], axis=1)
    w_all = jnp.concatenate([w_kv, wq[:, :o_ka], wq[:, o_qb:o_kb]], axis=1).astype(BF16)
    gq = jnp.tile(gqa_q_norm[0], 2)[None, :]
    gk = jnp.tile(gqa_k_norm[0], 2)[None, :]
    lane = jnp.arange(LANES)
    bd = (lane[:, None] // HEAD_DIM == lane[None, :] // HEAD_DIM).astype(BF16)
    cos, sin = _rope_tables(t)
    g_a = g_attn[0][None, :]

    ka, va, kb, vb, qa, qb = _proj_call(x, sh1, sc1, g_a, w_all, cos, sin, gq, gk, bd, rope=True, with_q=True)
    ka_c, va_c, kb_c, vb_c = _proj_call(ctx, csh1, csc1, g_a, w_all[:, :KV_COLS], cos, sin, gq, gk, bd,
                                        rope=False, with_q=False)

    keys = lambda ctx_part, lat_part: jnp.concatenate([ctx_part, lat_part], axis=2)
    att_a = _gqa_call(qa, keys(ka_c, ka), keys(va_c, va))
    att_d = _diff_call(qb, keys(kb_c, kb), keys(vb_c, vb), diff_lambda[0], diff_subln[0][None, :])

    wr_f32 = jnp.zeros((dm, LANES), F32).at[:, :N_EXPERTS].set(w_router[0])
    wr_hi = wr_f32.astype(BF16)
    wr_pad = jnp.stack([wr_hi, (wr_f32 - wr_hi.astype(F32)).astype(BF16)])
    br_pad = jnp.zeros((1, LANES), F32).at[0, :N_EXPERTS].set(b_router[0])
    x1, h2, idx, rank, gate_t, cnt = _oproj_call(
        att_a.reshape(n, -1), att_d.reshape(n, -1), x.reshape(n, dm), gt1, sh2, sc2, g_ffn[0][None, :],
        w_o[0].astype(BF16), wr_pad, br_pad, t // OPROJ_TM)

    blk = EXPERT_BLK
    counts = cnt[:, 0].astype(jnp.int32)
    padded = (counts + blk - 1) // blk * blk
    pad_ends = jnp.cumsum(padded)
    starts = pad_ends - padded
    n_blocks = n * TOP_K // blk + N_EXPERTS
    expert_ids = jnp.arange(N_EXPERTS, dtype=jnp.int32)
    dest = rank + jnp.sum(jnp.where(idx[..., None] == expert_ids, starts, 0), axis=-1)
    tiled = lambda tm: dest.reshape(TOP_K, n // tm, tm).transpose(1, 0, 2).reshape(-1)
    block_row0 = jnp.arange(n_blocks, dtype=jnp.int32) * blk
    block_exp = jnp.minimum(jnp.sum((pad_ends[None, :] <= block_row0[:, None]).astype(jnp.int32), axis=1),
                            N_EXPERTS - 1)
    n_used = (pad_ends[-1:] // blk).astype(jnp.int32)

    chunk = min(INVERT_CHUNK, t)
    ctm = min(COMBINE_TM, t)
    y = _experts_call(block_exp, n_used, counts, padded, starts, tiled(chunk), h2,
                      w_in[0], b_in[0], w_out[0], b_out[0], n_blocks * blk, chunk)
    out = _combine_call(y, x1, gate_t, gt2, g_final[None, :], t // ctm, ctm)
    return out.reshape(b, t, dm)
```

```python
import functools

import jax
import jax.numpy as jnp
from jax import lax
from jax.experimental import pallas as pl
from jax.experimental.pallas import tpu as pltpu

F32 = jnp.float32
BF16 = jnp.bfloat16

HEAD_DIM = 64
GQA_HEADS = 8
GQA_KV_HEADS = 2
GQA_GROUP = GQA_HEADS // GQA_KV_HEADS
DIFF_HEADS = 4
DIFF_V_DIM = 2 * HEAD_DIM
GRID_W = 64
ROPE_THETA = 10000.0
N_EXPERTS = 32
TOP_K = 4
SWIGLU_ALPHA = 1.702
SWIGLU_LIMIT = 7.0
N_MOD = 6
EPS = 1e-6
LAMBDA_INIT = 0.8 - 0.6
LOG2E = 1.4426950408889634
Q_SCALE = HEAD_DIM ** -0.5 * LOG2E

LANES = 128
SUBLANES = 8
GQA_V_EXT = LANES
KV_COLS = 2 * GQA_KV_HEADS * HEAD_DIM + DIFF_HEADS * 2 * HEAD_DIM + DIFF_HEADS * DIFF_V_DIM
Q_COLS = GQA_HEADS * HEAD_DIM + DIFF_HEADS * 2 * HEAD_DIM

PROJ_TM = 512
GQA_TQ = 256
DIFF_TQ = 512
ATT_TK = 2816
MXU_TILE = 256
OPROJ_TM = 512
INVERT_CHUNK = 1024
TOKEN_BITS = 15
TOKEN_MASK = (1 << TOKEN_BITS) - 1
COMBINE_TM = 512
EXPERT_BLK = 512


def _store_rows(ref, value):
    m = value.shape[0]
    for s in range(SUBLANES):
        ref[pl.ds(s, m, stride=SUBLANES), :] = value[:, s * LANES:(s + 1) * LANES]


def _load_rows(ref):
    m = ref.shape[0] // SUBLANES
    return jnp.concatenate([ref[pl.ds(s, m, stride=SUBLANES), :] for s in range(SUBLANES)], axis=1)


def _cparams(n_axes, vmem_mb):
    return pltpu.CompilerParams(dimension_semantics=("arbitrary",) * n_axes,
                                vmem_limit_bytes=vmem_mb * 1024 * 1024)


def _mod_kernel(cv_ref, w_ref, b_ref, o_ref):
    cv = cv_ref[...]
    s = cv * (1.0 / (1.0 + jnp.exp(-cv)))
    o_ref[...] = jnp.dot(s, w_ref[...], preferred_element_type=F32,
                         precision=lax.Precision.HIGHEST) + b_ref[...]


def _mod_call(cv, w_ada, b_ada):
    d, n = w_ada.shape
    tn = 1024
    return pl.pallas_call(
        _mod_kernel,
        grid=(n // tn,),
        in_specs=[pl.BlockSpec((8, d), lambda j: (0, 0)),
                  pl.BlockSpec((d, tn), lambda j: (0, j)),
                  pl.BlockSpec((1, tn), lambda j: (0, j))],
        out_specs=pl.BlockSpec((8, tn), lambda j: (0, j)),
        out_shape=jax.ShapeDtypeStruct((8, n), F32),
        compiler_params=_cparams(1, 32),
    )(cv, w_ada, b_ada.reshape(1, n))


def _head_rms(v, gain, bd):
    sq = v * v
    hi = sq.astype(BF16)
    lo = (sq - hi.astype(F32)).astype(BF16)
    ss = (jnp.dot(hi, bd, preferred_element_type=F32) + jnp.dot(lo, bd, preferred_element_type=F32))
    return v * lax.rsqrt(ss * (1.0 / HEAD_DIM) + EPS) * gain


def _rope_chunk(v, cos, sin, first_half):
    partner = jnp.where(first_half, pltpu.roll(v, LANES - 16, 1), pltpu.roll(v, 16, 1))
    return v * cos + partner * sin


def _proj_kernel(x_ref, sh_ref, sc_ref, g_ref, w_ref, cos_ref, sin_ref, gq_ref, gk_ref, bd_ref,
                 *out_refs, rope, with_q):
    xf = x_ref[0]
    ms = jnp.mean(xf * xf, axis=-1, keepdims=True)
    h = xf * lax.rsqrt(ms + EPS) * g_ref[...]
    h = h * (1.0 + sc_ref[0]) + sh_ref[0]
    p = jnp.dot(h.astype(BF16), w_ref[...], preferred_element_type=F32)

    if with_q:
        ka_ref, va_ref, kb_ref, vb_ref, qa_ref, qb_ref = out_refs
    else:
        ka_ref, va_ref, kb_ref, vb_ref = out_refs
    bd = bd_ref[...]
    if rope:
        cos = cos_ref[...]
        sin = sin_ref[...]
        lane = lax.broadcasted_iota(jnp.int32, cos.shape, 1)
        first_half = (lane % 32) < 16

    def chunk(j):
        return p[:, j * LANES:(j + 1) * LANES]

    def maybe_rope(v):
        return _rope_chunk(v, cos, sin, first_half) if rope else v

    def put_heads(ref, j, v):
        ref[0, 2 * j] = v[:, :HEAD_DIM].astype(ref.dtype)
        ref[0, 2 * j + 1] = v[:, HEAD_DIM:].astype(ref.dtype)

    put_heads(ka_ref, 0, maybe_rope(_head_rms(chunk(0), gk_ref[...], bd)))
    lanes = lax.broadcasted_iota(jnp.int32, (p.shape[0], LANES), 1)
    va = chunk(1)
    for h, v in enumerate((va, pltpu.roll(va, HEAD_DIM, 1))):
        va_ref[0, h] = jnp.where(lanes < HEAD_DIM, v, (lanes == HEAD_DIM).astype(F32)).astype(va_ref.dtype)
    for j in range(4):
        put_heads(kb_ref, j, maybe_rope(chunk(2 + j)))
        vb_ref[0, j] = chunk(6 + j).astype(vb_ref.dtype)
    if with_q:
        for j in range(4):
            put_heads(qa_ref, j, maybe_rope(_head_rms(chunk(10 + j), gq_ref[...], bd)) * Q_SCALE)
            put_heads(qb_ref, j, maybe_rope(chunk(14 + j)) * Q_SCALE)


def _proj_call(x, shift, scale, g, w, cos, sin, gq, gk, bd, *, rope, with_q):
    nb, t, d = x.shape
    tm = min(PROJ_TM, t)
    ncols = w.shape[1]
    heads = lambda n, w: pl.BlockSpec((1, n, tm, w), lambda b, i: (b, 0, i, 0))
    head = lambda n: heads(n, HEAD_DIM)
    out_specs = [head(2), heads(2, GQA_V_EXT), head(8), heads(4, DIFF_V_DIM)]
    out_shape = [jax.ShapeDtypeStruct((nb, 2, t, HEAD_DIM), BF16),
                 jax.ShapeDtypeStruct((nb, 2, t, GQA_V_EXT), BF16),
                 jax.ShapeDtypeStruct((nb, 8, t, HEAD_DIM), BF16),
                 jax.ShapeDtypeStruct((nb, 4, t, DIFF_V_DIM), BF16)]
    if with_q:
        out_specs += [head(8), head(8)]
        out_shape += [jax.ShapeDtypeStruct((nb, 8, t, HEAD_DIM), BF16)] * 2
    const = lambda shape: pl.BlockSpec(shape, lambda b, i: (0,) * len(shape))
    return pl.pallas_call(
        functools.partial(_proj_kernel, rope=rope, with_q=with_q),
        grid=(nb, t // tm),
        in_specs=[pl.BlockSpec((1, tm, d), lambda b, i: (b, i, 0)),
                  pl.BlockSpec((1, 1, d), lambda b, i: (b, 0, 0)),
                  pl.BlockSpec((1, 1, d), lambda b, i: (b, 0, 0)),
                  const((1, d)),
                  const((d, ncols)),
                  pl.BlockSpec((tm, LANES), lambda b, i: (i, 0)),
                  pl.BlockSpec((tm, LANES), lambda b, i: (i, 0)),
                  const((1, LANES)), const((1, LANES)), const((LANES, LANES))],
        out_specs=out_specs,
        out_shape=out_shape,
        compiler_params=_cparams(2, 48),
    )(x, shift, scale, g, w, cos, sin, gq, gk, bd)


def _scores(q, k):
    return lax.dot_general(q, k, (((1,), (1,)), ((), ())), preferred_element_type=F32)


def _softmax_step(s, v, carry, sum_on_mxu):
    m, l, acc = carry
    m_new = jnp.maximum(m, jnp.max(s, axis=1, keepdims=True))
    alpha = jnp.exp2(m - m_new)
    p = jnp.exp2(s - m_new)
    if not sum_on_mxu:
        l = alpha * l + jnp.sum(p, axis=1, keepdims=True)
    acc_new = alpha * acc + jnp.dot(p.astype(BF16), v, preferred_element_type=F32)
    return m_new, l, acc_new


def _init_carry(m_rows, dv):
    return (jnp.full((m_rows, 1), -jnp.inf, F32), jnp.zeros((m_rows, 1), F32), jnp.zeros((m_rows, dv), F32))


def _key_tile(n_keys):
    return max(tk for tk in range(MXU_TILE, min(ATT_TK, n_keys) + 1, MXU_TILE) if n_keys % tk == 0)


def _attend(scores, k_count, v_ref, m_rows, dv, tk):
    sum_on_mxu = v_ref.shape[3] > dv
    carry = _init_carry(m_rows, v_ref.shape[3])
    for j in range(k_count // tk):
        rows = pl.ds(j * tk, tk)
        carry = _softmax_step(scores(rows), v_ref[0, 0, rows, :], carry, sum_on_mxu)
    _, l, acc = carry
    return acc[:, :dv] / (acc[:, dv:dv + 1] if sum_on_mxu else l)


def _gqa_kernel(q_ref, k_ref, v_ref, o_ref, *, tk):
    tq = q_ref.shape[2]
    q = q_ref[0].reshape(GQA_GROUP * tq, HEAD_DIM)
    o = _attend(lambda rows: _scores(q, k_ref[0, 0, rows, :]), k_ref.shape[2], v_ref,
                GQA_GROUP * tq, HEAD_DIM, tk).reshape(GQA_GROUP, tq, HEAD_DIM)
    for g in range(GQA_GROUP):
        o_ref[0, :, g * HEAD_DIM:(g + 1) * HEAD_DIM] = o[g].astype(o_ref.dtype)


def _gqa_call(qa, ka, va):
    b, _, t, _ = qa.shape
    nk = ka.shape[2]
    tq = min(GQA_TQ, t)
    kv = lambda w: pl.BlockSpec((1, 1, nk, w), lambda bi, h, i: (bi, h, 0, 0))
    return pl.pallas_call(
        functools.partial(_gqa_kernel, tk=_key_tile(nk)),
        grid=(b, GQA_KV_HEADS, t // tq),
        in_specs=[pl.BlockSpec((1, GQA_GROUP, tq, HEAD_DIM), lambda bi, h, i: (bi, h, i, 0)),
                  kv(HEAD_DIM), kv(va.shape[3])],
        out_specs=pl.BlockSpec((1, tq, GQA_GROUP * HEAD_DIM), lambda bi, h, i: (bi, i, h)),
        out_shape=jax.ShapeDtypeStruct((b, t, GQA_HEADS * HEAD_DIM), BF16),
        compiler_params=_cparams(3, 48),
    )(qa, ka, va)


def _diff_kernel(q_ref, k_ref, v_ref, lam_ref, sg_ref, o_ref, *, tk):
    tq = q_ref.shape[2]
    q = (q_ref[0, 0], q_ref[0, 1])

    def scores(rows):
        return jnp.concatenate([_scores(q[c], k_ref[0, c, rows, :]) for c in range(2)], axis=0)

    o_both = _attend(scores, k_ref.shape[2], v_ref, 2 * tq, DIFF_V_DIM, tk)
    lamf = lam_ref[...]
    lam = (jnp.exp(jnp.sum(lamf[0:1] * lamf[1:2], axis=-1, keepdims=True))
           - jnp.exp(jnp.sum(lamf[2:3] * lamf[3:4], axis=-1, keepdims=True)) + LAMBDA_INIT)
    o = o_both[:tq] - lam * o_both[tq:]
    ms = jnp.mean(o * o, axis=-1, keepdims=True)
    o = o * lax.rsqrt(ms + EPS) * sg_ref[...] * (1.0 - LAMBDA_INIT)
    o_ref[0] = o.astype(o_ref.dtype)


def _diff_call(qb, kb, vb, lam, sub_g):
    b, _, t, _ = qb.shape
    nk = kb.shape[2]
    tq = min(DIFF_TQ, t)
    return pl.pallas_call(
        functools.partial(_diff_kernel, tk=_key_tile(nk)),
        grid=(b, DIFF_HEADS, t // tq),
        in_specs=[pl.BlockSpec((1, 2, tq, HEAD_DIM), lambda bi, h, i: (bi, h, i, 0)),
                  pl.BlockSpec((1, 2, nk, HEAD_DIM), lambda bi, h, i: (bi, h, 0, 0)),
                  pl.BlockSpec((1, 1, nk, vb.shape[3]), lambda bi, h, i: (bi, h, 0, 0)),
                  pl.BlockSpec((4, HEAD_DIM), lambda bi, h, i: (0, 0)),
                  pl.BlockSpec((1, DIFF_V_DIM), lambda bi, h, i: (0, 0))],
        out_specs=pl.BlockSpec((1, tq, DIFF_V_DIM), lambda bi, h, i: (bi, i, h)),
        out_shape=jax.ShapeDtypeStruct((b, t, DIFF_HEADS * DIFF_V_DIM), BF16),
        compiler_params=_cparams(3, 48),
    )(qb, kb, vb, lam, sub_g)


def _oproj_kernel(a_ref, d_ref, x_ref, gt_ref, sh_ref, sc_ref, g_ref, wo_ref, wr_ref, br_ref, tri_ref,
                  x1_ref, h2_ref, idx_ref, rank_ref, gate_ref, cnt_ref):
    i = pl.program_id(0)

    @pl.when(i == 0)
    def _():
        cnt_ref[...] = jnp.zeros_like(cnt_ref)

    half = a_ref.shape[1]
    mix = (jnp.dot(a_ref[...], wo_ref[0:half, :], preferred_element_type=F32)
           + jnp.dot(d_ref[...], wo_ref[half:2 * half, :], preferred_element_type=F32))
    x1 = x_ref[...] + gt_ref[0] * mix
    x1_ref[...] = x1
    ms = jnp.mean(x1 * x1, axis=-1, keepdims=True)
    h2 = x1 * lax.rsqrt(ms + EPS) * g_ref[...]
    h2 = h2 * (1.0 + sc_ref[0]) + sh_ref[0]
    _store_rows(h2_ref, h2)

    h_hi = h2.astype(BF16)
    h_lo = (h2 - h_hi.astype(F32)).astype(BF16)
    logits = (jnp.dot(h_hi, wr_ref[0], preferred_element_type=F32)
              + jnp.dot(h_lo, wr_ref[0], preferred_element_type=F32)
              + jnp.dot(h_hi, wr_ref[1], preferred_element_type=F32)) + br_ref[...]
    lt = logits.T[0:N_EXPERTS, :]
    tm = lt.shape[1]
    eid = lax.broadcasted_iota(jnp.int32, lt.shape, 0).astype(F32)
    vals, sels = [], []
    for k in range(TOP_K):
        m = jnp.max(lt, axis=0, keepdims=True)
        first = jnp.min(jnp.where(lt == m, eid, float(N_EXPERTS)), axis=0, keepdims=True)
        sel = eid == first
        lt = jnp.where(sel, -jnp.inf, lt)
        vals.append(m)
        sels.append(sel)
        idx_ref[k:k + 1, :] = first.astype(jnp.int32)

    onehot = sum(s.astype(F32) for s in sels)
    before = cnt_ref[:, 0:1] + jnp.dot(onehot.astype(BF16), tri_ref[...], preferred_element_type=F32)
    for k in range(TOP_K):
        rank_ref[k:k + 1, :] = jnp.sum(jnp.where(sels[k], before, 0.0), axis=0, keepdims=True).astype(jnp.int32)
    cnt_ref[...] = cnt_ref[...] + jnp.sum(onehot, axis=1, keepdims=True)

    es = [jnp.exp(v - vals[0]) for v in vals]
    den = es[0] + es[1] + es[2] + es[3]
    gates = jnp.concatenate([e / den for e in es] + [jnp.zeros((LANES - TOP_K, tm), F32)], axis=0)
    gate_ref[...] = gates.T


def _oproj_call(a, d, x, gt1, sh2, sc2, g_ffn, w_o, wr_pad, br_pad, tiles_per_batch):
    n, dm = x.shape
    tm = OPROJ_TM
    tri = (lax.broadcasted_iota(jnp.int32, (tm, tm), 0) < lax.broadcasted_iota(jnp.int32, (tm, tm), 1)).astype(BF16)
    row = lambda w: pl.BlockSpec((tm, w), lambda i: (i, 0))
    per_batch = pl.BlockSpec((1, 1, dm), lambda i: (i // tiles_per_batch, 0, 0))
    const = lambda shape: pl.BlockSpec(shape, lambda i: (0,) * len(shape))
    return pl.pallas_call(
        _oproj_kernel,
        grid=(n // tm,),
        in_specs=[row(a.shape[1]), row(d.shape[1]), row(dm), per_batch, per_batch, per_batch,
                  const((1, dm)), const(w_o.shape), const(wr_pad.shape), const((1, LANES)), const((tm, tm))],
        out_specs=[row(dm), pl.BlockSpec((tm * SUBLANES, LANES), lambda i: (i, 0)),
                   pl.BlockSpec((TOP_K, tm), lambda i: (0, i)),
                   pl.BlockSpec((TOP_K, tm), lambda i: (0, i)),
                   row(LANES),
                   const((N_EXPERTS, LANES))],
        out_shape=[jax.ShapeDtypeStruct((n, dm), F32),
                   jax.ShapeDtypeStruct((n * SUBLANES, LANES), F32),
                   jax.ShapeDtypeStruct((TOP_K, n), jnp.int32),
                   jax.ShapeDtypeStruct((TOP_K, n), jnp.int32),
                   jax.ShapeDtypeStruct((n, LANES), F32),
                   jax.ShapeDtypeStruct((N_EXPERTS, LANES), F32)],
        compiler_params=_cparams(1, 48),
    )(a, d, x, gt1, sh2, sc2, g_ffn, w_o, wr_pad, br_pad, tri)


def _row_copy(src_ref, src_row, dst_ref, dst_row, sem):
    tile = lambda r: pl.ds(pl.multiple_of(r * SUBLANES, SUBLANES), SUBLANES)
    return pltpu.make_async_copy(src_ref.at[tile(src_row)], dst_ref.at[tile(dst_row)], sem)


def _experts_kernel(bexp_ref, nused_ref, cnt_ref, pad_ref, start_ref, dest_hbm, h2_hbm, win_ref, bin_ref, wout_ref,
                    bout_ref, y_hbm, win_bf, wout_bf, row_map, dest_buf, landing, x_bf, staged, sem_i, sem_g, sem_s,
                    *, chunk, n_tok):
    i = pl.program_id(0)
    blk = EXPERT_BLK
    n_used = nused_ref[0]

    def gather_start(block, r):
        _row_copy(h2_hbm, row_map[block * blk + r] & TOKEN_MASK, landing, r, sem_g).start()

    def gather_wait():
        _row_copy(h2_hbm, 0, landing, 0, sem_g).wait()

    def scatter_start(block, r):
        _row_copy(staged, r, y_hbm, lax.shift_right_logical(row_map[block * blk + r], TOKEN_BITS), sem_s).start()

    def scatter_wait():
        _row_copy(staged, 0, y_hbm, 0, sem_s).wait()

    @pl.when(i == 0)
    def _():
        staged[...] = jnp.zeros_like(staged)
        spare = pltpu.make_async_copy(staged, y_hbm.at[pl.ds(TOP_K * n_tok * SUBLANES, blk * SUBLANES)], sem_s)
        spare.start()
        spare.wait()
        for e in range(N_EXPERTS):
            lo = start_ref[e] + cnt_ref[e]
            hi = start_ref[e] + pad_ref[e]

            def mark_padding(r, c):
                row_map[r] = (TOP_K * n_tok + lax.rem(r, blk)) << TOKEN_BITS
                return c

            lax.fori_loop(lo, hi, mark_padding, 0)

        def invert_chunk(c, carry):
            copy = pltpu.make_async_copy(dest_hbm.at[pl.ds(c * (TOP_K * chunk), TOP_K * chunk)], dest_buf, sem_i)
            copy.start()
            copy.wait()

            def per_token(t, c2):
                for k in range(TOP_K):
                    tok = c * chunk + t
                    row_map[dest_buf[k * chunk + t]] = ((k * n_tok + tok) << TOKEN_BITS) | tok
                return c2

            return lax.fori_loop(0, chunk, per_token, carry)

        lax.fori_loop(0, dest_hbm.shape[0] // (TOP_K * chunk), invert_chunk, 0)

        def first(r, c):
            gather_start(0, r)
            return c

        lax.fori_loop(0, blk, first, 0)

    prev = bexp_ref[jnp.maximum(i - 1, 0)]
    new_expert = jnp.logical_or(i == 0, bexp_ref[i] != prev)

    @pl.when(jnp.logical_and(i < n_used, new_expert))
    def _():
        win_bf[...] = win_ref[0].astype(BF16)
        wout_bf[...] = wout_ref[0].astype(BF16)

    @pl.when(i < n_used)
    def _():
        dff = wout_bf.shape[0]
        for r in range(blk):
            gather_wait()
        x_bf[...] = _load_rows(landing).astype(BF16)
        nxt = jnp.minimum(i + 1, n_used - 1)
        done = jnp.maximum(i - 1, 0)

        def start_rows(r, c):
            gather_start(nxt, r)
            scatter_start(done, r)
            return c

        lax.fori_loop(0, blk, start_rows, 0, unroll=8)
        hcat = jnp.dot(x_bf[...], win_bf[...], preferred_element_type=F32) + bin_ref[0]
        x_glu = jnp.minimum(hcat[:, :dff], SWIGLU_LIMIT)
        x_lin = jnp.clip(hcat[:, dff:], -SWIGLU_LIMIT, SWIGLU_LIMIT)
        act = x_glu * (1.0 / (1.0 + jnp.exp(-SWIGLU_ALPHA * x_glu))) * (x_lin + 1.0)
        y = jnp.dot(act.astype(BF16), wout_bf[...], preferred_element_type=F32) + bout_ref[0]
        for r in range(blk):
            scatter_wait()
        _store_rows(staged, y)

    @pl.when(i == n_used - 1)
    def _():
        for r in range(blk):
            gather_wait()

    @pl.when(i == n_used)
    def _():
        def last(r, c):
            scatter_start(n_used - 1, r)
            return c

        lax.fori_loop(0, blk, last, 0)
        for r in range(blk):
            scatter_wait()


def _experts_call(block_exp, n_used, counts, padded, starts, dest_flat, h2, w_in, b_in, w_out, b_out, n_rows, chunk):
    ne, dm, dff2 = w_in.shape
    dff = dff2 // 2
    blk = EXPERT_BLK
    n_tok = h2.shape[0] // SUBLANES
    assert n_rows // blk > (n_tok * TOP_K) // blk, "needs a grid step after the last used block"
    assert n_tok <= TOKEN_MASK and (TOP_K * n_tok + blk) <= (1 << (32 - TOKEN_BITS)), "row_map word layout"
    any_spec = pl.BlockSpec(memory_space=pl.ANY)
    by_expert = lambda shape: pl.BlockSpec((1,) + shape, lambda i, be, *_: (be[i], 0, 0))
    return pl.pallas_call(
        functools.partial(_experts_kernel, chunk=chunk, n_tok=n_tok),
        grid_spec=pltpu.PrefetchScalarGridSpec(
            num_scalar_prefetch=5,
            grid=(n_rows // blk,),
            in_specs=[any_spec, any_spec,
                      by_expert((dm, dff2)), by_expert((1, dff2)), by_expert((dff, dm)), by_expert((1, dm))],
            out_specs=any_spec,
            scratch_shapes=[pltpu.VMEM((dm, dff2), BF16), pltpu.VMEM((dff, dm), BF16),
                            pltpu.SMEM((n_rows,), jnp.int32), pltpu.SMEM((TOP_K * chunk,), jnp.int32),
                            pltpu.VMEM((blk * SUBLANES, LANES), F32), pltpu.VMEM((blk, dm), BF16),
                            pltpu.VMEM((blk * SUBLANES, LANES), F32),
                            pltpu.SemaphoreType.DMA(()), pltpu.SemaphoreType.DMA(()), pltpu.SemaphoreType.DMA(())]),
        out_shape=jax.ShapeDtypeStruct(((TOP_K * n_tok + blk) * SUBLANES, LANES), F32),
        compiler_params=_cparams(1, 56),
    )(block_exp, n_used, counts, padded, starts, dest_flat, h2,
      w_in, b_in.reshape(ne, 1, dff2), w_out, b_out.reshape(ne, 1, dm))


def _combine_kernel(*refs):
    y_refs, (x1_ref, gate_ref, gt_ref, gf_ref, o_ref) = refs[:TOP_K], refs[TOP_K:]
    gates = gate_ref[...]
    y = gates[:, 0:1] * _load_rows(y_refs[0])
    for k in range(1, TOP_K):
        y = y + gates[:, k:k + 1] * _load_rows(y_refs[k])
    x2 = x1_ref[...] + gt_ref[0] * y
    ms = jnp.mean(x2 * x2, axis=-1, keepdims=True)
    o_ref[...] = x2 * lax.rsqrt(ms + EPS) * gf_ref[...]


def _combine_call(y, x1, gate_t, gt2, g_final, tiles_per_batch, tm):
    n, dm = x1.shape
    tiles = n // tm
    row = lambda w: pl.BlockSpec((tm, w), lambda i: (i, 0))
    expert_out = lambda k: pl.BlockSpec((tm * SUBLANES, LANES), lambda i: (k * tiles + i, 0))
    return pl.pallas_call(
        _combine_kernel,
        grid=(tiles,),
        in_specs=[expert_out(k) for k in range(TOP_K)] + [
            row(dm), row(LANES),
            pl.BlockSpec((1, 1, dm), lambda i: (i // tiles_per_batch, 0, 0)),
            pl.BlockSpec((1, dm), lambda i: (0, 0))],
        out_specs=row(dm),
        out_shape=jax.ShapeDtypeStruct((n, dm), F32),
        compiler_params=_cparams(1, 40),
    )(y, y, y, y, x1, gate_t, gt2, g_final)


def _rope_tables(t):
    rows = t // GRID_W
    n_freq = HEAD_DIM // 4
    inv_freq = ROPE_THETA ** (-jnp.arange(n_freq, dtype=F32) / n_freq)
    ar = jnp.arange(rows, dtype=F32)[:, None] * inv_freq
    ac = jnp.arange(GRID_W, dtype=F32)[:, None] * inv_freq
    per_row = lambda a: jnp.broadcast_to(a[:, None, :], (rows, GRID_W, n_freq))
    per_col = lambda a: jnp.broadcast_to(a[None, :, :], (rows, GRID_W, n_freq))
    cr, sr, cc, sc = per_row(jnp.cos(ar)), per_row(jnp.sin(ar)), per_col(jnp.cos(ac)), per_col(jnp.sin(ac))
    cos = jnp.concatenate([cr, cr, cc, cc] * 2, axis=-1).reshape(t, LANES)
    sin = jnp.concatenate([-sr, sr, -sc, sc] * 2, axis=-1).reshape(t, LANES)
    return cos, sin


def kernel(x, c, ctx, c_ctx, w_ada, b_ada, g_attn, w_qkv, gqa_q_norm, gqa_k_norm, diff_lambda,
           diff_subln, w_o, g_ffn, w_router, b_router, w_in, b_in, w_out, b_out, g_final):
    assert w_ada.shape[0] == 1, "single-layer block"
    b, t, dm = x.shape
    n = b * t

    cv = jnp.zeros((8, dm), F32).at[:b].set(c).at[b].set(c_ctx)
    mod = _mod_call(cv, w_ada[0], b_ada[0])
    sh1, sc1, gt1, sh2, sc2, gt2 = [m[:b, None, :] for m in jnp.split(mod, N_MOD, axis=-1)]
    csh1, csc1 = [jnp.broadcast_to(m[b][None, None, :], (b, 1, dm)) for m in jnp.split(mod, N_MOD, axis=-1)[:2]]

    wq = w_qkv[0]
    o_ka, o_va, o_qb, o_kb, o_vb = 512, 640, 768, 1280, 1792
    w_kv = jnp.concatenate([wq[:, o_ka:o_qb], wq[:, o_kb:]], axis=1)
    w_all = jnp.concatenate([w_kv, wq[:, :o_ka], wq[:, o_qb:o_kb]], axis=1).astype(BF16)
    gq = jnp.tile(gqa_q_norm[0], 2)[None, :]
    gk = jnp.tile(gqa_k_norm[0], 2)[None, :]
    lane = jnp.arange(LANES)
    bd = (lane[:, None] // HEAD_DIM == lane[None, :] // HEAD_DIM).astype(BF16)
    cos, sin = _rope_tables(t)
    g_a = g_attn[0][None, :]

    ka, va, kb, vb, qa, qb = _proj_call(x, sh1, sc1, g_a, w_all, cos, sin, gq, gk, bd, rope=True, with_q=True)
    ka_c, va_c, kb_c, vb_c = _proj_call(ctx, csh1, csc1, g_a, w_all[:, :KV_COLS], cos, sin, gq, gk, bd,
                                        rope=False, with_q=False)

    keys = lambda ctx_part, lat_part: jnp.concatenate([ctx_part, lat_part], axis=2)
    att_a = _gqa_call(qa, keys(ka_c, ka), keys(va_c, va))
    att_d = _diff_call(qb, keys(kb_c, kb), keys(vb_c, vb), diff_lambda[0], diff_subln[0][None, :])

    wr_f32 = jnp.zeros((dm, LANES), F32).at[:, :N_EXPERTS].set(w_router[0])
    wr_hi = wr_f32.astype(BF16)
    wr_pad = jnp.stack([wr_hi, (wr_f32 - wr_hi.astype(F32)).astype(BF16)])
    br_pad = jnp.zeros((1, LANES), F32).at[0, :N_EXPERTS].set(b_router[0])
    x1, h2, idx, rank, gate_t, cnt = _oproj_call(
        att_a.reshape(n, -1), att_d.reshape(n, -1), x.reshape(n, dm), gt1, sh2, sc2, g_ffn[0][None, :],
        w_o[0].astype(BF16), wr_pad, br_pad, t // OPROJ_TM)

    blk = EXPERT_BLK
    counts = cnt[:, 0].astype(jnp.int32)
    padded = (counts + blk - 1) // blk * blk
    pad_ends = jnp.cumsum(padded)
    starts = pad_ends - padded
    n_blocks = n * TOP_K // blk + N_EXPERTS
    expert_ids = jnp.arange(N_EXPERTS, dtype=jnp.int32)
    dest = rank + jnp.sum(jnp.where(idx[..., None] == expert_ids, starts, 0), axis=-1)
    tiled = lambda tm: dest.reshape(TOP_K, n // tm, tm).transpose(1, 0, 2).reshape(-1)
    block_row0 = jnp.arange(n_blocks, dtype=jnp.int32) * blk
    block_exp = jnp.minimum(jnp.sum((pad_ends[None, :] <= block_row0[:, None]).astype(jnp.int32), axis=1),
                            N_EXPERTS - 1)
    n_used = (pad_ends[-1:] // blk).astype(jnp.int32)

    chunk = min(INVERT_CHUNK, t)
    ctm = min(COMBINE_TM, t)
    y = _experts_call(block_exp, n_used, counts, padded, starts, tiled(chunk), h2,
                      w_in[0], b_in[0], w_out[0], b_out[0], n_blocks * blk, chunk)
    out = _combine_call(y, x1, gate_t, gt2, g_final[None, :], t // ctm, ctm)
    return out.reshape(b, t, dm)
```

```python
import functools

import jax
import jax.numpy as jnp
from jax import lax
from jax.experimental import pallas as pl
from jax.experimental.pallas import tpu as pltpu

F32 = jnp.float32
BF16 = jnp.bfloat16

HEAD_DIM = 64
GQA_HEADS = 8
GQA_KV_HEADS = 2
GQA_GROUP = GQA_HEADS // GQA_KV_HEADS
DIFF_HEADS = 4
DIFF_V_DIM = 2 * HEAD_DIM
GRID_W = 64
ROPE_THETA = 10000.0
N_EXPERTS = 32
TOP_K = 4
SWIGLU_ALPHA = 1.702
SWIGLU_LIMIT = 7.0
N_MOD = 6
EPS = 1e-6
LAMBDA_INIT = 0.8 - 0.6
LOG2E = 1.4426950408889634
Q_SCALE = HEAD_DIM ** -0.5 * LOG2E

LANES = 128
SUBLANES = 8
GQA_V_EXT = LANES
KV_COLS = 2 * GQA_KV_HEADS * HEAD_DIM + DIFF_HEADS * 2 * HEAD_DIM + DIFF_HEADS * DIFF_V_DIM
Q_COLS = GQA_HEADS * HEAD_DIM + DIFF_HEADS * 2 * HEAD_DIM

PROJ_TM = 512
GQA_TQ = 256
DIFF_TQ = 512
ATT_TK = 2816
MXU_TILE = 256
OPROJ_TM = 512
INVERT_CHUNK = 1024
TOKEN_BITS = 15
TOKEN_MASK = (1 << TOKEN_BITS) - 1
COMBINE_TM = 512
EXPERT_BLK = 512


def _store_rows(ref, value):
    m = value.shape[0]
    for s in range(SUBLANES):
        ref[pl.ds(s, m, stride=SUBLANES), :] = value[:, s * LANES:(s + 1) * LANES]


def _load_rows(ref):
    m = ref.shape[0] // SUBLANES
    return jnp.concatenate([ref[pl.ds(s, m, stride=SUBLANES), :] for s in range(SUBLANES)], axis=1)


def _cparams(n_axes, vmem_mb):
    return pltpu.CompilerParams(dimension_semantics=("arbitrary",) * n_axes,
                                vmem_limit_bytes=vmem_mb * 1024 * 1024)


def _mod_kernel(cv_ref, w_ref, b_ref, o_ref):
    cv = cv_ref[...]
    s = cv * (1.0 / (1.0 + jnp.exp(-cv)))
    o_ref[...] = jnp.dot(s, w_ref[...], preferred_element_type=F32,
                         precision=lax.Precision.HIGHEST) + b_ref[...]


def _mod_call(cv, w_ada, b_ada):
    d, n = w_ada.shape
    tn = 1024
    return pl.pallas_call(
        _mod_kernel,
        grid=(n // tn,),
        in_specs=[pl.BlockSpec((8, d), lambda j: (0, 0)),
                  pl.BlockSpec((d, tn), lambda j: (0, j)),
                  pl.BlockSpec((1, tn), lambda j: (0, j))],
        out_specs=pl.BlockSpec((8, tn), lambda j: (0, j)),
        out_shape=jax.ShapeDtypeStruct((8, n), F32),
        compiler_params=_cparams(1, 32),
    )(cv, w_ada, b_ada.reshape(1, n))


def _head_rms(v, gain, bd):
    sq = v * v
    hi = sq.astype(BF16)
    lo = (sq - hi.astype(F32)).astype(BF16)
    ss = (jnp.dot(hi, bd, preferred_element_type=F32) + jnp.dot(lo, bd, preferred_element_type=F32))
    return v * lax.rsqrt(ss * (1.0 / HEAD_DIM) + EPS) * gain


def _rope_chunk(v, cos, sin, first_half):
    partner = jnp.where(first_half, pltpu.roll(v, LANES - 16, 1), pltpu.roll(v, 16, 1))
    return v * cos + partner * sin


def _proj_kernel(x_ref, sh_ref, sc_ref, g_ref, w_ref, cos_ref, sin_ref, gq_ref, gk_ref, bd_ref,
                 *out_refs, rope, with_q):
    xf = x_ref[0]
    ms = jnp.mean(xf * xf, axis=-1, keepdims=True)
    h = xf * lax.rsqrt(ms + EPS) * g_ref[...]
    h = h * (1.0 + sc_ref[0]) + sh_ref[0]
    p = jnp.dot(h.astype(BF16), w_ref[...], preferred_element_type=F32)

    if with_q:
        ka_ref, va_ref, kb_ref, vb_ref, qa_ref, qb_ref = out_refs
    else:
        ka_ref, va_ref, kb_ref, vb_ref = out_refs
    bd = bd_ref[...]
    if rope:
        cos = cos_ref[...]
        sin = sin_ref[...]
        lane = lax.broadcasted_iota(jnp.int32, cos.shape, 1)
        first_half = (lane % 32) < 16

    def chunk(j):
        return p[:, j * LANES:(j + 1) * LANES]

    def maybe_rope(v):
        return _rope_chunk(v, cos, sin, first_half) if rope else v

    def put_heads(ref, j, v):
        ref[0, 2 * j] = v[:, :HEAD_DIM].astype(ref.dtype)
        ref[0, 2 * j + 1] = v[:, HEAD_DIM:].astype(ref.dtype)

    put_heads(ka_ref, 0, maybe_rope(_head_rms(chunk(0), gk_ref[...], bd)))
    lanes = lax.broadcasted_iota(jnp.int32, (p.shape[0], LANES), 1)
    va = chunk(1)
    for h, v in enumerate((va, pltpu.roll(va, HEAD_DIM, 1))):
        va_ref[0, h] = jnp.where(lanes < HEAD_DIM, v, (lanes == HEAD_DIM).astype(F32)).astype(va_ref.dtype)
    for j in range(4):
        put_heads(kb_ref, j, maybe_rope(chunk(2 + j)))
        vb_ref[0, j] = chunk(6 + j).astype(vb_ref.dtype)
    if with_q:
        for j in range(4):
            put_heads(qa_ref, j, maybe_rope(_head_rms(chunk(10 + j), gq_ref[...], bd)) * Q_SCALE)
            put_heads(qb_ref, j, maybe_rope(chunk(14 + j)) * Q_SCALE)


def _proj_call(x, shift, scale, g, w, cos, sin, gq, gk, bd, *, rope, with_q):
    nb, t, d = x.shape
    tm = min(PROJ_TM, t)
    ncols = w.shape[1]
    heads = lambda n, w: pl.BlockSpec((1, n, tm, w), lambda b, i: (b, 0, i, 0))
    head = lambda n: heads(n, HEAD_DIM)
    out_specs = [head(2), heads(2, GQA_V_EXT), head(8), heads(4, DIFF_V_DIM)]
    out_shape = [jax.ShapeDtypeStruct((nb, 2, t, HEAD_DIM), BF16),
                 jax.ShapeDtypeStruct((nb, 2, t, GQA_V_EXT), BF16),
                 jax.ShapeDtypeStruct((nb, 8, t, HEAD_DIM), BF16),
                 jax.ShapeDtypeStruct((nb, 4, t, DIFF_V_DIM), BF16)]
    if with_q:
        out_specs += [head(8), head(8)]
        out_shape += [jax.ShapeDtypeStruct((nb, 8, t, HEAD_DIM), BF16)] * 2
    const = lambda shape: pl.BlockSpec(shape, lambda b, i: (0,) * len(shape))
    return pl.pallas_call(
        functools.partial(_proj_kernel, rope=rope, with_q=with_q),
        grid=(nb, t // tm),
        in_specs=[pl.BlockSpec((1, tm, d), lambda b, i: (b, i, 0)),
                  pl.BlockSpec((1, 1, d), lambda b, i: (b, 0, 0)),
                  pl.BlockSpec((1, 1, d), lambda b, i: (b, 0, 0)),
                  const((1, d)),
                  const((d, ncols)),
                  pl.BlockSpec((tm, LANES), lambda b, i: (i, 0)),
                  pl.BlockSpec((tm, LANES), lambda b, i: (i, 0)),
                  const((1, LANES)), const((1, LANES)), const((LANES, LANES))],
        out_specs=out_specs,
        out_shape=out_shape,
        compiler_params=_cparams(2, 48),
    )(x, shift, scale, g, w, cos, sin, gq, gk, bd)


def _scores(q, k):
    return lax.dot_general(q, k, (((1,), (1,)), ((), ())), preferred_element_type=F32)


def _softmax_step(s, v, carry, sum_on_mxu):
    m, l, acc = carry
    m_new = jnp.maximum(m, jnp.max(s, axis=1, keepdims=True))
    alpha = jnp.exp2(m - m_new)
    p = jnp.exp2(s - m_new)
    if not sum_on_mxu:
        l = alpha * l + jnp.sum(p, axis=1, keepdims=True)
    acc_new = alpha * acc + jnp.dot(p.astype(BF16), v, preferred_element_type=F32)
    return m_new, l, acc_new


def _init_carry(m_rows, dv):
    return (jnp.full((m_rows, 1), -jnp.inf, F32), jnp.zeros((m_rows, 1), F32), jnp.zeros((m_rows, dv), F32))


def _key_tile(n_keys):
    return max(tk for tk in range(MXU_TILE, min(ATT_TK, n_keys) + 1, MXU_TILE) if n_keys % tk == 0)


def _attend(scores, k_count, v_ref, m_rows, dv, tk):
    sum_on_mxu = v_ref.shape[3] > dv
    carry = _init_carry(m_rows, v_ref.shape[3])
    for j in range(k_count // tk):
        rows = pl.ds(j * tk, tk)
        carry = _softmax_step(scores(rows), v_ref[0, 0, rows, :], carry, sum_on_mxu)
    _, l, acc = carry
    return acc[:, :dv] / (acc[:, dv:dv + 1] if sum_on_mxu else l)


def _gqa_kernel(q_ref, k_ref, v_ref, o_ref, *, tk):
    tq = q_ref.shape[2]
    q = q_ref[0].reshape(GQA_GROUP * tq, HEAD_DIM)
    o = _attend(lambda rows: _scores(q, k_ref[0, 0, rows, :]), k_ref.shape[2], v_ref,
                GQA_GROUP * tq, HEAD_DIM, tk).reshape(GQA_GROUP, tq, HEAD_DIM)
    for g in range(GQA_GROUP):
        o_ref[0, :, g * HEAD_DIM:(g + 1) * HEAD_DIM] = o[g].astype(o_ref.dtype)


def _gqa_call(qa, ka, va):
    b, _, t, _ = qa.shape
    nk = ka.shape[2]
    tq = min(GQA_TQ, t)
    kv = lambda w: pl.BlockSpec((1, 1, nk, w), lambda bi, h, i: (bi, h, 0, 0))
    return pl.pallas_call(
        functools.partial(_gqa_kernel, tk=_key_tile(nk)),
        grid=(b, GQA_KV_HEADS, t // tq),
        in_specs=[pl.BlockSpec((1, GQA_GROUP, tq, HEAD_DIM), lambda bi, h, i: (bi, h, i, 0)),
                  kv(HEAD_DIM), kv(va.shape[3])],
        out_specs=pl.BlockSpec((1, tq, GQA_GROUP * HEAD_DIM), lambda bi, h, i: (bi, i, h)),
        out_shape=jax.ShapeDtypeStruct((b, t, GQA_HEADS * HEAD_DIM), BF16),
        compiler_params=_cparams(3, 48),
    )(qa, ka, va)


def _diff_kernel(q_ref, k_ref, v_ref, lam_ref, sg_ref, o_ref, *, tk):
    tq = q_ref.shape[2]
    q = (q_ref[0, 0], q_ref[0, 1])

    def scores(rows):
        return jnp.concatenate([_scores(q[c], k_ref[0, c, rows, :]) for c in range(2)], axis=0)

    o_both = _attend(scores, k_ref.shape[2], v_ref, 2 * tq, DIFF_V_DIM, tk)
    lamf = lam_ref[...]
    lam = (jnp.exp(jnp.sum(lamf[0:1] * lamf[1:2], axis=-1, keepdims=True))
           - jnp.exp(jnp.sum(lamf[2:3] * lamf[3:4], axis=-1, keepdims=True)) + LAMBDA_INIT)
    o = o_both[:tq] - lam * o_both[tq:]
    ms = jnp.mean(o * o, axis=-1, keepdims=True)
    o = o * lax.rsqrt(ms + EPS) * sg_ref[...] * (1.0 - LAMBDA_INIT)
    o_ref[0] = o.astype(o_ref.dtype)


def _diff_call(qb, kb, vb, lam, sub_g):
    b, _, t, _ = qb.shape
    nk = kb.shape[2]
    tq = min(DIFF_TQ, t)
    return pl.pallas_call(
        functools.partial(_diff_kernel, tk=_key_tile(nk)),
        grid=(b, DIFF_HEADS, t // tq),
        in_specs=[pl.BlockSpec((1, 2, tq, HEAD_DIM), lambda bi, h, i: (bi, h, i, 0)),
                  pl.BlockSpec((1, 2, nk, HEAD_DIM), lambda bi, h, i: (bi, h, 0, 0)),
                  pl.BlockSpec((1, 1, nk, vb.shape[3]), lambda bi, h, i: (bi, h, 0, 0)),
                  pl.BlockSpec((4, HEAD_DIM), lambda bi, h, i: (0, 0)),
                  pl.BlockSpec((1, DIFF_V_DIM), lambda bi, h, i: (0, 0))],
        out_specs=pl.BlockSpec((1, tq, DIFF_V_DIM), lambda bi, h, i: (bi, i, h)),
        out_shape=jax.ShapeDtypeStruct((b, t, DIFF_HEADS * DIFF_V_DIM), BF16),
        compiler_params=_cparams(3, 48),
    )(qb, kb, vb, lam, sub_g)


def _oproj_kernel(a_ref, d_ref, x_ref, gt_ref, sh_ref, sc_ref, g_ref, wo_ref, wr_ref, br_ref, tri_ref,
                  x1_ref, h2_ref, idx_ref, rank_ref, gate_ref, cnt_ref):
    i = pl.program_id(0)

    @pl.when(i == 0)
    def _():
        cnt_ref[...] = jnp.zeros_like(cnt_ref)

    half = a_ref.shape[1]
    mix = (jnp.dot(a_ref[...], wo_ref[0:half, :], preferred_element_type=F32)
           + jnp.dot(d_ref[...], wo_ref[half:2 * half, :], preferred_element_type=F32))
    x1 = x_ref[...] + gt_ref[0] * mix
    x1_ref[...] = x1
    ms = jnp.mean(x1 * x1, axis=-1, keepdims=True)
    h2 = x1 * lax.rsqrt(ms + EPS) * g_ref[...]
    h2 = h2 * (1.0 + sc_ref[0]) + sh_ref[0]
    _store_rows(h2_ref, h2)

    h_hi = h2.astype(BF16)
    h_lo = (h2 - h_hi.astype(F32)).astype(BF16)
    logits = (jnp.dot(h_hi, wr_ref[0], preferred_element_type=F32)
              + jnp.dot(h_lo, wr_ref[0], preferred_element_type=F32)
              + jnp.dot(h_hi, wr_ref[1], preferred_element_type=F32)) + br_ref[...]
    lt = logits.T[0:N_EXPERTS, :]
    tm = lt.shape[1]
    eid = lax.broadcasted_iota(jnp.int32, lt.shape, 0).astype(F32)
    vals, sels = [], []
    for k in range(TOP_K):
        m = jnp.max(lt, axis=0, keepdims=True)
        first = jnp.min(jnp.where(lt == m, eid, float(N_EXPERTS)), axis=0, keepdims=True)
        sel = eid == first
        lt = jnp.where(sel, -jnp.inf, lt)
        vals.append(m)
        sels.append(sel)
        idx_ref[k:k + 1, :] = first.astype(jnp.int32)

    onehot = sum(s.astype(F32) for s in sels)
    before = cnt_ref[:, 0:1] + jnp.dot(onehot.astype(BF16), tri_ref[...], preferred_element_type=F32)
    for k in range(TOP_K):
        rank_ref[k:k + 1, :] = jnp.sum(jnp.where(sels[k], before, 0.0), axis=0, keepdims=True).astype(jnp.int32)
    cnt_ref[...] = cnt_ref[...] + jnp.sum(onehot, axis=1, keepdims=True)

    es = [jnp.exp(v - vals[0]) for v in vals]
    den = es[0] + es[1] + es[2] + es[3]
    gates = jnp.concatenate([e / den for e in es] + [jnp.zeros((LANES - TOP_K, tm), F32)], axis=0)
    gate_ref[...] = gates.T


def _oproj_call(a, d, x, gt1, sh2, sc2, g_ffn, w_o, wr_pad, br_pad, tiles_per_batch):
    n, dm = x.shape
    tm = OPROJ_TM
    tri = (lax.broadcasted_iota(jnp.int32, (tm, tm), 0) < lax.broadcasted_iota(jnp.int32, (tm, tm), 1)).astype(BF16)
    row = lambda w: pl.BlockSpec((tm, w), lambda i: (i, 0))
    per_batch = pl.BlockSpec((1, 1, dm), lambda i: (i // tiles_per_batch, 0, 0))
    const = lambda shape: pl.BlockSpec(shape, lambda i: (0,) * len(shape))
    return pl.pallas_call(
        _oproj_kernel,
        grid=(n // tm,),
        in_specs=[row(a.shape[1]), row(d.shape[1]), row(dm), per_batch, per_batch, per_batch,
                  const((1, dm)), const(w_o.shape), const(wr_pad.shape), const((1, LANES)), const((tm, tm))],
        out_specs=[row(dm), pl.BlockSpec((tm * SUBLANES, LANES), lambda i: (i, 0)),
                   pl.BlockSpec((TOP_K, tm), lambda i: (0, i)),
                   pl.BlockSpec((TOP_K, tm), lambda i: (0, i)),
                   row(LANES),
                   const((N_EXPERTS, LANES))],
        out_shape=[jax.ShapeDtypeStruct((n, dm), F32),
                   jax.ShapeDtypeStruct((n * SUBLANES, LANES), F32),
                   jax.ShapeDtypeStruct((TOP_K, n), jnp.int32),
                   jax.ShapeDtypeStruct((TOP_K, n), jnp.int32),
                   jax.ShapeDtypeStruct((n, LANES), F32),
                   jax.ShapeDtypeStruct((N_EXPERTS, LANES), F32)],
        compiler_params=_cparams(1, 48),
    )(a, d, x, gt1, sh2, sc2, g_ffn, w_o, wr_pad, br_pad, tri)


def _row_copy(src_ref, src_row, dst_ref, dst_row, sem):
    tile = lambda r: pl.ds(pl.multiple_of(r * SUBLANES, SUBLANES), SUBLANES)
    return pltpu.make_async_copy(src_ref.at[tile(src_row)], dst_ref.at[tile(dst_row)], sem)


def _experts_kernel(bexp_ref, nused_ref, cnt_ref, pad_ref, start_ref, dest_hbm, h2_hbm, win_ref, bin_ref, wout_ref,
                    bout_ref, y_hbm, win_bf, wout_bf, row_map, dest_buf, landing, x_bf, staged, sem_i, sem_g, sem_s,
                    *, chunk, n_tok):
    i = pl.program_id(0)
    blk = EXPERT_BLK
    n_used = nused_ref[0]

    def gather_start(block, r):
        _row_copy(h2_hbm, row_map[block * blk + r] & TOKEN_MASK, landing, r, sem_g).start()

    def gather_wait():
        _row_copy(h2_hbm, 0, landing, 0, sem_g).wait()

    def scatter_start(block, r):
        _row_copy(staged, r, y_hbm, lax.shift_right_logical(row_map[block * blk + r], TOKEN_BITS),
                  sem_s).start(priority=1)

    def scatter_wait():
        _row_copy(staged, 0, y_hbm, 0, sem_s).wait()

    @pl.when(i == 0)
    def _():
        staged[...] = jnp.zeros_like(staged)
        spare = pltpu.make_async_copy(staged, y_hbm.at[pl.ds(TOP_K * n_tok * SUBLANES, blk * SUBLANES)], sem_s)
        spare.start()
        spare.wait()
        for e in range(N_EXPERTS):
            lo = start_ref[e] + cnt_ref[e]
            hi = start_ref[e] + pad_ref[e]

            def mark_padding(r, c):
                row_map[r] = (TOP_K * n_tok + lax.rem(r, blk)) << TOKEN_BITS
                return c

            lax.fori_loop(lo, hi, mark_padding, 0)

        def invert_chunk(c, carry):
            copy = pltpu.make_async_copy(dest_hbm.at[pl.ds(c * (TOP_K * chunk), TOP_K * chunk)], dest_buf, sem_i)
            copy.start()
            copy.wait()

            def per_token(t, c2):
                for k in range(TOP_K):
                    tok = c * chunk + t
                    row_map[dest_buf[k * chunk + t]] = ((k * n_tok + tok) << TOKEN_BITS) | tok
                return c2

            return lax.fori_loop(0, chunk, per_token, carry)

        lax.fori_loop(0, dest_hbm.shape[0] // (TOP_K * chunk), invert_chunk, 0)

        def first(r, c):
            gather_start(0, r)
            return c

        lax.fori_loop(0, blk, first, 0)

    prev = bexp_ref[jnp.maximum(i - 1, 0)]
    new_expert = jnp.logical_or(i == 0, bexp_ref[i] != prev)

    @pl.when(jnp.logical_and(i < n_used, new_expert))
    def _():
        win_bf[...] = win_ref[0].astype(BF16)
        wout_bf[...] = wout_ref[0].astype(BF16)

    @pl.when(i < n_used)
    def _():
        dff = wout_bf.shape[0]
        for r in range(blk):
            gather_wait()
        x_bf[...] = _load_rows(landing).astype(BF16)
        nxt = jnp.minimum(i + 1, n_used - 1)
        done = jnp.maximum(i - 1, 0)

        def start_rows(r, c):
            gather_start(nxt, r)
            scatter_start(done, r)
            return c

        lax.fori_loop(0, blk, start_rows, 0, unroll=8)
        hcat = jnp.dot(x_bf[...], win_bf[...], preferred_element_type=F32) + bin_ref[0]
        x_glu = jnp.minimum(hcat[:, :dff], SWIGLU_LIMIT)
        x_lin = jnp.clip(hcat[:, dff:], -SWIGLU_LIMIT, SWIGLU_LIMIT)
        act = x_glu * (1.0 / (1.0 + jnp.exp(-SWIGLU_ALPHA * x_glu))) * (x_lin + 1.0)
        y = jnp.dot(act.astype(BF16), wout_bf[...], preferred_element_type=F32) + bout_ref[0]
        for r in range(blk):
            scatter_wait()
        _store_rows(staged, y)

    @pl.when(i == n_used - 1)
    def _():
        for r in range(blk):
            gather_wait()

    @pl.when(i == n_used)
    def _():
        def last(r, c):
            scatter_start(n_used - 1, r)
            return c

        lax.fori_loop(0, blk, last, 0)
        for r in range(blk):
            scatter_wait()


def _experts_call(block_exp, n_used, counts, padded, starts, dest_flat, h2, w_in, b_in, w_out, b_out, n_rows, chunk):
    ne, dm, dff2 = w_in.shape
    dff = dff2 // 2
    blk = EXPERT_BLK
    n_tok = h2.shape[0] // SUBLANES
    assert n_rows // blk > (n_tok * TOP_K) // blk, "needs a grid step after the last used block"
    assert n_tok <= TOKEN_MASK and (TOP_K * n_tok + blk) <= (1 << (32 - TOKEN_BITS)), "row_map word layout"
    any_spec = pl.BlockSpec(memory_space=pl.ANY)
    by_expert = lambda shape: pl.BlockSpec((1,) + shape, lambda i, be, *_: (be[i], 0, 0))
    return pl.pallas_call(
        functools.partial(_experts_kernel, chunk=chunk, n_tok=n_tok),
        grid_spec=pltpu.PrefetchScalarGridSpec(
            num_scalar_prefetch=5,
            grid=(n_rows // blk,),
            in_specs=[any_spec, any_spec,
                      by_expert((dm, dff2)), by_expert((1, dff2)), by_expert((dff, dm)), by_expert((1, dm))],
            out_specs=any_spec,
            scratch_shapes=[pltpu.VMEM((dm, dff2), BF16), pltpu.VMEM((dff, dm), BF16),
                            pltpu.SMEM((n_rows,), jnp.int32), pltpu.SMEM((TOP_K * chunk,), jnp.int32),
                            pltpu.VMEM((blk * SUBLANES, LANES), F32), pltpu.VMEM((blk, dm), BF16),
                            pltpu.VMEM((blk * SUBLANES, LANES), F32),
                            pltpu.SemaphoreType.DMA(()), pltpu.SemaphoreType.DMA(()), pltpu.SemaphoreType.DMA(())]),
        out_shape=jax.ShapeDtypeStruct(((TOP_K * n_tok + blk) * SUBLANES, LANES), F32),
        compiler_params=_cparams(1, 56),
    )(block_exp, n_used, counts, padded, starts, dest_flat, h2,
      w_in, b_in.reshape(ne, 1, dff2), w_out, b_out.reshape(ne, 1, dm))


def _combine_kernel(*refs):
    y_refs, (x1_ref, gate_ref, gt_ref, gf_ref, o_ref) = refs[:TOP_K], refs[TOP_K:]
    gates = gate_ref[...]
    y = gates[:, 0:1] * _load_rows(y_refs[0])
    for k in range(1, TOP_K):
        y = y + gates[:, k:k + 1] * _load_rows(y_refs[k])
    x2 = x1_ref[...] + gt_ref[0] * y
    ms = jnp.mean(x2 * x2, axis=-1, keepdims=True)
    o_ref[...] = x2 * lax.rsqrt(ms + EPS) * gf_ref[...]


def _combine_call(y, x1, gate_t, gt2, g_final, tiles_per_batch, tm):
    n, dm = x1.shape
    tiles = n // tm
    row = lambda w: pl.BlockSpec((tm, w), lambda i: (i, 0))
    expert_out = lambda k: pl.BlockSpec((tm * SUBLANES, LANES), lambda i: (k * tiles + i, 0))
    return pl.pallas_call(
        _combine_kernel,
        grid=(tiles,),
        in_specs=[expert_out(k) for k in range(TOP_K)] + [
            row(dm), row(LANES),
            pl.BlockSpec((1, 1, dm), lambda i: (i // tiles_per_batch, 0, 0)),
            pl.BlockSpec((1, dm), lambda i: (0, 0))],
        out_specs=row(dm),
        out_shape=jax.ShapeDtypeStruct((n, dm), F32),
        compiler_params=_cparams(1, 40),
    )(y, y, y, y, x1, gate_t, gt2, g_final)


def _rope_tables(t):
    rows = t // GRID_W
    n_freq = HEAD_DIM // 4
    inv_freq = ROPE_THETA ** (-jnp.arange(n_freq, dtype=F32) / n_freq)
    ar = jnp.arange(rows, dtype=F32)[:, None] * inv_freq
    ac = jnp.arange(GRID_W, dtype=F32)[:, None] * inv_freq
    per_row = lambda a: jnp.broadcast_to(a[:, None, :], (rows, GRID_W, n_freq))
    per_col = lambda a: jnp.broadcast_to(a[None, :, :], (rows, GRID_W, n_freq))
    cr, sr, cc, sc = per_row(jnp.cos(ar)), per_row(jnp.sin(ar)), per_col(jnp.cos(ac)), per_col(jnp.sin(ac))
    cos = jnp.concatenate([cr, cr, cc, cc] * 2, axis=-1).reshape(t, LANES)
    sin = jnp.concatenate([-sr, sr, -sc, sc] * 2, axis=-1).reshape(t, LANES)
    return cos, sin


def kernel(x, c, ctx, c_ctx, w_ada, b_ada, g_attn, w_qkv, gqa_q_norm, gqa_k_norm, diff_lambda,
           diff_subln, w_o, g_ffn, w_router, b_router, w_in, b_in, w_out, b_out, g_final):
    assert w_ada.shape[0] == 1, "single-layer block"
    b, t, dm = x.shape
    n = b * t

    cv = jnp.zeros((8, dm), F32).at[:b].set(c).at[b].set(c_ctx)
    mod = _mod_call(cv, w_ada[0], b_ada[0])
    sh1, sc1, gt1, sh2, sc2, gt2 = [m[:b, None, :] for m in jnp.split(mod, N_MOD, axis=-1)]
    csh1, csc1 = [jnp.broadcast_to(m[b][None, None, :], (b, 1, dm)) for m in jnp.split(mod, N_MOD, axis=-1)[:2]]

    wq = w_qkv[0]
    o_ka, o_va, o_qb, o_kb, o_vb = 512, 640, 768, 1280, 1792
    w_kv = jnp.concatenate([wq[:, o_ka:o_qb], wq[:, o_kb:]], axis=1)
    w_all = jnp.concatenate([w_kv, wq[:, :o_ka], wq[:, o_qb:o_kb]], axis=1).astype(BF16)
    gq = jnp.tile(gqa_q_norm[0], 2)[None, :]
    gk = jnp.tile(gqa_k_norm[0], 2)[None, :]
    lane = jnp.arange(LANES)
    bd = (lane[:, None] // HEAD_DIM == lane[None, :] // HEAD_DIM).astype(BF16)
    cos, sin = _rope_tables(t)
    g_a = g_attn[0][None, :]

    ka, va, kb, vb, qa, qb = _proj_call(x, sh1, sc1, g_a, w_all, cos, sin, gq, gk, bd, rope=True, with_q=True)
    ka_c, va_c, kb_c, vb_c = _proj_call(ctx, csh1, csc1, g_a, w_all[:, :KV_COLS], cos, sin, gq, gk, bd,
                                        rope=False, with_q=False)

    keys = lambda ctx_part, lat_part: jnp.concatenate([ctx_part, lat_part], axis=2)
    att_a = _gqa_call(qa, keys(ka_c, ka), keys(va_c, va))
    att_d = _diff_call(qb, keys(kb_c, kb), keys(vb_c, vb), diff_lambda[0], diff_subln[0][None, :])

    wr_f32 = jnp.zeros((dm, LANES), F32).at[:, :N_EXPERTS].set(w_router[0])
    wr_hi = wr_f32.astype(BF16)
    wr_pad = jnp.stack([wr_hi, (wr_f32 - wr_hi.astype(F32)).astype(BF16)])
    br_pad = jnp.zeros((1, LANES), F32).at[0, :N_EXPERTS].set(b_router[0])
    x1, h2, idx, rank, gate_t, cnt = _oproj_call(
        att_a.reshape(n, -1), att_d.reshape(n, -1), x.reshape(n, dm), gt1, sh2, sc2, g_ffn[0][None, :],
        w_o[0].astype(BF16), wr_pad, br_pad, t // OPROJ_TM)

    blk = EXPERT_BLK
    counts = cnt[:, 0].astype(jnp.int32)
    padded = (counts + blk - 1) // blk * blk
    pad_ends = jnp.cumsum(padded)
    starts = pad_ends - padded
    n_blocks = n * TOP_K // blk + N_EXPERTS
    expert_ids = jnp.arange(N_EXPERTS, dtype=jnp.int32)
    dest = rank + jnp.sum(jnp.where(idx[..., None] == expert_ids, starts, 0), axis=-1)
    tiled = lambda tm: dest.reshape(TOP_K, n // tm, tm).transpose(1, 0, 2).reshape(-1)
    block_row0 = jnp.arange(n_blocks, dtype=jnp.int32) * blk
    block_exp = jnp.minimum(jnp.sum((pad_ends[None, :] <= block_row0[:, None]).astype(jnp.int32), axis=1),
                            N_EXPERTS - 1)
    n_used = (pad_ends[-1:] // blk).astype(jnp.int32)

    chunk = min(INVERT_CHUNK, t)
    ctm = min(COMBINE_TM, t)
    y = _experts_call(block_exp, n_used, counts, padded, starts, tiled(chunk), h2,
                      w_in[0], b_in[0], w_out[0], b_out[0], n_blocks * blk, chunk)
    out = _combine_call(y, x1, gate_t, gt2, g_final[None, :], t // ctm, ctm)
    return out.reshape(b, t, dm)
```

```python
import functools

import jax
import jax.numpy as jnp
import numpy as np
from jax import lax
from jax.experimental import pallas as pl
from jax.experimental.pallas import tpu as pltpu

F32 = jnp.float32
BF16 = jnp.bfloat16

HEAD_DIM = 64
GQA_HEADS = 8
GQA_KV_HEADS = 2
GQA_GROUP = GQA_HEADS // GQA_KV_HEADS
DIFF_HEADS = 4
DIFF_V_DIM = 2 * HEAD_DIM
GRID_W = 64
ROPE_THETA = 10000.0
N_EXPERTS = 32
TOP_K = 4
SWIGLU_ALPHA = 1.702
SWIGLU_LIMIT = 7.0
N_MOD = 6
EPS = 1e-6
LAMBDA_INIT = 0.8 - 0.6
LOG2E = 1.4426950408889634
Q_SCALE = HEAD_DIM ** -0.5 * LOG2E

LANES = 128
SUBLANES = 8
GQA_V_EXT = LANES
KV_COLS = 2 * GQA_KV_HEADS * HEAD_DIM + DIFF_HEADS * 2 * HEAD_DIM + DIFF_HEADS * DIFF_V_DIM
Q_COLS = GQA_HEADS * HEAD_DIM + DIFF_HEADS * 2 * HEAD_DIM

PROJ_TM = 512
GQA_TQ = 256
DIFF_TQ = 512
ATT_TK = 2816
MXU_TILE = 256
OPROJ_TM = 512
DISPATCH_TM = 512
COMBINE_TM = 256
EXPERT_BLK = 512


def _store_rows(ref, value):
    m = value.shape[0]
    for s in range(SUBLANES):
        ref[pl.ds(s, m, stride=SUBLANES), :] = value[:, s * LANES:(s + 1) * LANES]


def _load_rows(ref):
    m = ref.shape[0] // SUBLANES
    return jnp.concatenate([ref[pl.ds(s, m, stride=SUBLANES), :] for s in range(SUBLANES)], axis=1)


def _cparams(n_axes, vmem_mb):
    return pltpu.CompilerParams(dimension_semantics=("arbitrary",) * n_axes,
                                vmem_limit_bytes=vmem_mb * 1024 * 1024)


def _mod_kernel(cv_ref, w_ref, b_ref, o_ref):
    cv = cv_ref[...]
    s = cv * (1.0 / (1.0 + jnp.exp(-cv)))
    o_ref[...] = jnp.dot(s, w_ref[...], preferred_element_type=F32,
                         precision=lax.Precision.HIGHEST) + b_ref[...]


def _mod_call(cv, w_ada, b_ada):
    d, n = w_ada.shape
    tn = 1024
    return pl.pallas_call(
        _mod_kernel,
        grid=(n // tn,),
        in_specs=[pl.BlockSpec((8, d), lambda j: (0, 0)),
                  pl.BlockSpec((d, tn), lambda j: (0, j)),
                  pl.BlockSpec((1, tn), lambda j: (0, j))],
        out_specs=pl.BlockSpec((8, tn), lambda j: (0, j)),
        out_shape=jax.ShapeDtypeStruct((8, n), F32),
        compiler_params=_cparams(1, 32),
    )(cv, w_ada, b_ada.reshape(1, n))


def _head_rms(v, gain, bd):
    sq = v * v
    hi = sq.astype(BF16)
    lo = (sq - hi.astype(F32)).astype(BF16)
    ss = (jnp.dot(hi, bd, preferred_element_type=F32) + jnp.dot(lo, bd, preferred_element_type=F32))
    return v * lax.rsqrt(ss * (1.0 / HEAD_DIM) + EPS) * gain


def _rope_chunk(v, cos, sin, first_half):
    partner = jnp.where(first_half, pltpu.roll(v, LANES - 16, 1), pltpu.roll(v, 16, 1))
    return v * cos + partner * sin


def _proj_kernel(x_ref, sh_ref, sc_ref, g_ref, w_ref, cos_ref, sin_ref, gq_ref, gk_ref, bd_ref,
                 *out_refs, rope, with_q):
    xf = x_ref[0]
    ms = jnp.mean(xf * xf, axis=-1, keepdims=True)
    h = xf * lax.rsqrt(ms + EPS) * g_ref[...]
    h = h * (1.0 + sc_ref[0]) + sh_ref[0]
    p = jnp.dot(h.astype(BF16), w_ref[...], preferred_element_type=F32)

    if with_q:
        ka_ref, va_ref, kb_ref, vb_ref, qa_ref, qb_ref = out_refs
    else:
        ka_ref, va_ref, kb_ref, vb_ref = out_refs
    bd = bd_ref[...]
    if rope:
        cos = cos_ref[...]
        sin = sin_ref[...]
        lane = lax.broadcasted_iota(jnp.int32, cos.shape, 1)
        first_half = (lane % 32) < 16

    def chunk(j):
        return p[:, j * LANES:(j + 1) * LANES]

    def maybe_rope(v):
        return _rope_chunk(v, cos, sin, first_half) if rope else v

    def put_heads(ref, j, v):
        ref[0, 2 * j] = v[:, :HEAD_DIM].astype(ref.dtype)
        ref[0, 2 * j + 1] = v[:, HEAD_DIM:].astype(ref.dtype)

    put_heads(ka_ref, 0, maybe_rope(_head_rms(chunk(0), gk_ref[...], bd)))
    lanes = lax.broadcasted_iota(jnp.int32, (p.shape[0], LANES), 1)
    va = chunk(1)
    for h, v in enumerate((va, pltpu.roll(va, HEAD_DIM, 1))):
        va_ref[0, h] = jnp.where(lanes < HEAD_DIM, v, (lanes == HEAD_DIM).astype(F32)).astype(va_ref.dtype)
    for j in range(4):
        put_heads(kb_ref, j, maybe_rope(chunk(2 + j)))
        vb_ref[0, j] = chunk(6 + j).astype(vb_ref.dtype)
    if with_q:
        for j in range(4):
            put_heads(qa_ref, j, maybe_rope(_head_rms(chunk(10 + j), gq_ref[...], bd)) * Q_SCALE)
            put_heads(qb_ref, j, maybe_rope(chunk(14 + j)) * Q_SCALE)


def _proj_call(x, shift, scale, g, w, cos, sin, gq, gk, bd, *, rope, with_q):
    nb, t, d = x.shape
    tm = min(PROJ_TM, t)
    ncols = w.shape[1]
    heads = lambda n, w: pl.BlockSpec((1, n, tm, w), lambda b, i: (b, 0, i, 0))
    head = lambda n: heads(n, HEAD_DIM)
    out_specs = [head(2), heads(2, GQA_V_EXT), head(8), heads(4, DIFF_V_DIM)]
    out_shape = [jax.ShapeDtypeStruct((nb, 2, t, HEAD_DIM), BF16),
                 jax.ShapeDtypeStruct((nb, 2, t, GQA_V_EXT), BF16),
                 jax.ShapeDtypeStruct((nb, 8, t, HEAD_DIM), BF16),
                 jax.ShapeDtypeStruct((nb, 4, t, DIFF_V_DIM), BF16)]
    if with_q:
        out_specs += [head(8), head(8)]
        out_shape += [jax.ShapeDtypeStruct((nb, 8, t, HEAD_DIM), BF16)] * 2
    const = lambda shape: pl.BlockSpec(shape, lambda b, i: (0,) * len(shape))
    return pl.pallas_call(
        functools.partial(_proj_kernel, rope=rope, with_q=with_q),
        grid=(nb, t // tm),
        in_specs=[pl.BlockSpec((1, tm, d), lambda b, i: (b, i, 0)),
                  pl.BlockSpec((1, 1, d), lambda b, i: (b, 0, 0)),
                  pl.BlockSpec((1, 1, d), lambda b, i: (b, 0, 0)),
                  const((1, d)),
                  const((d, ncols)),
                  pl.BlockSpec((tm, LANES), lambda b, i: (i, 0)),
                  pl.BlockSpec((tm, LANES), lambda b, i: (i, 0)),
                  const((1, LANES)), const((1, LANES)), const((LANES, LANES))],
        out_specs=out_specs,
        out_shape=out_shape,
        compiler_params=_cparams(2, 48),
    )(x, shift, scale, g, w, cos, sin, gq, gk, bd)


def _scores(q, k):
    return lax.dot_general(q, k, (((1,), (1,)), ((), ())), preferred_element_type=F32)


def _softmax_step(s, v, carry, sum_on_mxu):
    m, l, acc = carry
    m_new = jnp.maximum(m, jnp.max(s, axis=1, keepdims=True))
    alpha = jnp.exp2(m - m_new)
    p = jnp.exp2(s - m_new)
    if not sum_on_mxu:
        l = alpha * l + jnp.sum(p, axis=1, keepdims=True)
    acc_new = alpha * acc + jnp.dot(p.astype(BF16), v, preferred_element_type=F32)
    return m_new, l, acc_new


def _init_carry(m_rows, dv):
    return (jnp.full((m_rows, 1), -jnp.inf, F32), jnp.zeros((m_rows, 1), F32), jnp.zeros((m_rows, dv), F32))


def _key_tile(n_keys):
    return max(tk for tk in range(MXU_TILE, min(ATT_TK, n_keys) + 1, MXU_TILE) if n_keys % tk == 0)


def _attend(scores, k_count, v_ref, m_rows, dv, tk):
    sum_on_mxu = v_ref.shape[3] > dv
    carry = _init_carry(m_rows, v_ref.shape[3])
    for j in range(k_count // tk):
        rows = pl.ds(j * tk, tk)
        carry = _softmax_step(scores(rows), v_ref[0, 0, rows, :], carry, sum_on_mxu)
    _, l, acc = carry
    return acc[:, :dv] / (acc[:, dv:dv + 1] if sum_on_mxu else l)


def _gqa_kernel(q_ref, k_ref, v_ref, o_ref, *, tk):
    tq = q_ref.shape[2]
    q = q_ref[0].reshape(GQA_GROUP * tq, HEAD_DIM)
    o = _attend(lambda rows: _scores(q, k_ref[0, 0, rows, :]), k_ref.shape[2], v_ref,
                GQA_GROUP * tq, HEAD_DIM, tk).reshape(GQA_GROUP, tq, HEAD_DIM)
    for g in range(GQA_GROUP):
        o_ref[0, :, g * HEAD_DIM:(g + 1) * HEAD_DIM] = o[g].astype(o_ref.dtype)


def _gqa_call(qa, ka, va):
    b, _, t, _ = qa.shape
    nk = ka.shape[2]
    tq = min(GQA_TQ, t)
    kv = lambda w: pl.BlockSpec((1, 1, nk, w), lambda bi, h, i: (bi, h, 0, 0))
    return pl.pallas_call(
        functools.partial(_gqa_kernel, tk=_key_tile(nk)),
        grid=(b, GQA_KV_HEADS, t // tq),
        in_specs=[pl.BlockSpec((1, GQA_GROUP, tq, HEAD_DIM), lambda bi, h, i: (bi, h, i, 0)),
                  kv(HEAD_DIM), kv(va.shape[3])],
        out_specs=pl.BlockSpec((1, tq, GQA_GROUP * HEAD_DIM), lambda bi, h, i: (bi, i, h)),
        out_shape=jax.ShapeDtypeStruct((b, t, GQA_HEADS * HEAD_DIM), BF16),
        compiler_params=_cparams(3, 48),
    )(qa, ka, va)


def _diff_kernel(q_ref, k_ref, v_ref, lam_ref, sg_ref, o_ref, *, tk):
    tq = q_ref.shape[2]
    q = (q_ref[0, 0], q_ref[0, 1])

    def scores(rows):
        return jnp.concatenate([_scores(q[c], k_ref[0, c, rows, :]) for c in range(2)], axis=0)

    o_both = _attend(scores, k_ref.shape[2], v_ref, 2 * tq, DIFF_V_DIM, tk)
    lamf = lam_ref[...]
    lam = (jnp.exp(jnp.sum(lamf[0:1] * lamf[1:2], axis=-1, keepdims=True))
           - jnp.exp(jnp.sum(lamf[2:3] * lamf[3:4], axis=-1, keepdims=True)) + LAMBDA_INIT)
    o = o_both[:tq] - lam * o_both[tq:]
    ms = jnp.mean(o * o, axis=-1, keepdims=True)
    o = o * lax.rsqrt(ms + EPS) * sg_ref[...] * (1.0 - LAMBDA_INIT)
    o_ref[0] = o.astype(o_ref.dtype)


def _diff_call(qb, kb, vb, lam, sub_g):
    b, _, t, _ = qb.shape
    nk = kb.shape[2]
    tq = min(DIFF_TQ, t)
    return pl.pallas_call(
        functools.partial(_diff_kernel, tk=_key_tile(nk)),
        grid=(b, DIFF_HEADS, t // tq),
        in_specs=[pl.BlockSpec((1, 2, tq, HEAD_DIM), lambda bi, h, i: (bi, h, i, 0)),
                  pl.BlockSpec((1, 2, nk, HEAD_DIM), lambda bi, h, i: (bi, h, 0, 0)),
                  pl.BlockSpec((1, 1, nk, vb.shape[3]), lambda bi, h, i: (bi, h, 0, 0)),
                  pl.BlockSpec((4, HEAD_DIM), lambda bi, h, i: (0, 0)),
                  pl.BlockSpec((1, DIFF_V_DIM), lambda bi, h, i: (0, 0))],
        out_specs=pl.BlockSpec((1, tq, DIFF_V_DIM), lambda bi, h, i: (bi, i, h)),
        out_shape=jax.ShapeDtypeStruct((b, t, DIFF_HEADS * DIFF_V_DIM), BF16),
        compiler_params=_cparams(3, 48),
    )(qb, kb, vb, lam, sub_g)


def _oproj_kernel(a_ref, d_ref, x_ref, gt_ref, sh_ref, sc_ref, g_ref, wo_ref, wr_ref, br_ref, tri_ref,
                  x1_ref, h2_ref, idx_ref, rank_ref, gate_ref, cnt_ref):
    i = pl.program_id(0)

    @pl.when(i == 0)
    def _():
        cnt_ref[...] = jnp.zeros_like(cnt_ref)

    half = a_ref.shape[1]
    mix = (jnp.dot(a_ref[...], wo_ref[0:half, :], preferred_element_type=F32)
           + jnp.dot(d_ref[...], wo_ref[half:2 * half, :], preferred_element_type=F32))
    x1 = x_ref[...] + gt_ref[0] * mix
    x1_ref[...] = x1
    ms = jnp.mean(x1 * x1, axis=-1, keepdims=True)
    h2 = x1 * lax.rsqrt(ms + EPS) * g_ref[...]
    h2 = h2 * (1.0 + sc_ref[0]) + sh_ref[0]
    _store_rows(h2_ref, h2)

    h_hi = h2.astype(BF16)
    h_lo = (h2 - h_hi.astype(F32)).astype(BF16)
    logits = (jnp.dot(h_hi, wr_ref[0], preferred_element_type=F32)
              + jnp.dot(h_lo, wr_ref[0], preferred_element_type=F32)
              + jnp.dot(h_hi, wr_ref[1], preferred_element_type=F32)) + br_ref[...]
    lt = logits.T[0:N_EXPERTS, :]
    tm = lt.shape[1]
    eid = lax.broadcasted_iota(jnp.int32, lt.shape, 0).astype(F32)
    vals, sels = [], []
    for k in range(TOP_K):
        m = jnp.max(lt, axis=0, keepdims=True)
        first = jnp.min(jnp.where(lt == m, eid, float(N_EXPERTS)), axis=0, keepdims=True)
        sel = eid == first
        lt = jnp.where(sel, -jnp.inf, lt)
        vals.append(m)
        sels.append(sel)
        idx_ref[k:k + 1, :] = first.astype(jnp.int32)

    onehot = sum(s.astype(F32) for s in sels)
    before = cnt_ref[:, 0:1] + jnp.dot(onehot.astype(BF16), tri_ref[...], preferred_element_type=F32)
    for k in range(TOP_K):
        rank_ref[k:k + 1, :] = jnp.sum(jnp.where(sels[k], before, 0.0), axis=0, keepdims=True).astype(jnp.int32)
    cnt_ref[...] = cnt_ref[...] + jnp.sum(onehot, axis=1, keepdims=True)

    es = [jnp.exp(v - vals[0]) for v in vals]
    den = es[0] + es[1] + es[2] + es[3]
    gates = jnp.concatenate([e / den for e in es] + [jnp.zeros((LANES - TOP_K, tm), F32)], axis=0)
    gate_ref[...] = gates.T


def _oproj_call(a, d, x, gt1, sh2, sc2, g_ffn, w_o, wr_pad, br_pad, tiles_per_batch):
    n, dm = x.shape
    tm = OPROJ_TM
    tri = (lax.broadcasted_iota(jnp.int32, (tm, tm), 0) < lax.broadcasted_iota(jnp.int32, (tm, tm), 1)).astype(BF16)
    row = lambda w: pl.BlockSpec((tm, w), lambda i: (i, 0))
    per_batch = pl.BlockSpec((1, 1, dm), lambda i: (i // tiles_per_batch, 0, 0))
    const = lambda shape: pl.BlockSpec(shape, lambda i: (0,) * len(shape))
    return pl.pallas_call(
        _oproj_kernel,
        grid=(n // tm,),
        in_specs=[row(a.shape[1]), row(d.shape[1]), row(dm), per_batch, per_batch, per_batch,
                  const((1, dm)), const(w_o.shape), const(wr_pad.shape), const((1, LANES)), const((tm, tm))],
        out_specs=[row(dm), pl.BlockSpec((tm * SUBLANES, LANES), lambda i: (i, 0)),
                   pl.BlockSpec((TOP_K, tm), lambda i: (0, i)),
                   pl.BlockSpec((TOP_K, tm), lambda i: (0, i)),
                   row(LANES),
                   const((N_EXPERTS, LANES))],
        out_shape=[jax.ShapeDtypeStruct((n, dm), F32),
                   jax.ShapeDtypeStruct((n * SUBLANES, LANES), F32),
                   jax.ShapeDtypeStruct((TOP_K, n), jnp.int32),
                   jax.ShapeDtypeStruct((TOP_K, n), jnp.int32),
                   jax.ShapeDtypeStruct((n, LANES), F32),
                   jax.ShapeDtypeStruct((N_EXPERTS, LANES), F32)],
        compiler_params=_cparams(1, 48),
    )(a, d, x, gt1, sh2, sc2, g_ffn, w_o, wr_pad, br_pad, tri)


def _row_copy(src_ref, src_row, dst_ref, dst_row, sem):
    tile = lambda r: pl.ds(pl.multiple_of(r * SUBLANES, SUBLANES), SUBLANES)
    return pltpu.make_async_copy(src_ref.at[tile(src_row)], dst_ref.at[tile(dst_row)], sem)


def _dispatch_kernel(cnt_ref, pad_ref, start_ref, nused_ref, dest_hbm, h2_ref, xs_hbm,
                     idx_smem, zeros, sem_i, sem_d, sem_z, *, tm, blk):
    i = pl.program_id(0)

    @pl.when(i == 0)
    def _():
        zeros[...] = jnp.zeros_like(zeros)
        for e in range(N_EXPERTS):
            lo = start_ref[e] + cnt_ref[e]
            hi = start_ref[e] + pad_ref[e]
            lax.fori_loop(lo, hi, lambda r, c: (_row_copy(zeros, 0, xs_hbm, r, sem_z).start(), c)[1], 0)
        for e in range(N_EXPERTS):
            lo = start_ref[e] + cnt_ref[e]
            hi = start_ref[e] + pad_ref[e]
            lax.fori_loop(lo, hi, lambda r, c: (_row_copy(zeros, 0, xs_hbm, r, sem_z).wait(), c)[1], 0)

        def tail_copy(j):
            rows = pl.ds(pl.multiple_of(j * (blk * SUBLANES), blk * SUBLANES), blk * SUBLANES)
            return pltpu.make_async_copy(zeros, xs_hbm.at[rows], sem_z)

        n_blocks = xs_hbm.shape[0] // (blk * SUBLANES)
        lax.fori_loop(nused_ref[0], n_blocks, lambda j, c: (tail_copy(j).start(), c)[1], 0)
        lax.fori_loop(nused_ref[0], n_blocks, lambda j, c: (tail_copy(j).wait(), c)[1], 0)

    idx_copy = pltpu.make_async_copy(dest_hbm.at[pl.ds(i * (TOP_K * tm), TOP_K * tm)], idx_smem, sem_i)
    idx_copy.start()
    idx_copy.wait()

    def issue(t, c):
        for k in range(TOP_K):
            _row_copy(h2_ref, t, xs_hbm, idx_smem[k * tm + t], sem_d).start(priority=k % 2)
        return c

    def drain(t, c):
        for k in range(TOP_K):
            _row_copy(h2_ref, 0, xs_hbm, 0, sem_d).wait()
        return c

    lax.fori_loop(0, tm, issue, 0)
    lax.fori_loop(0, tm, drain, 0)


def _dispatch_call(counts, padded, starts, n_used, dest_flat, h2, n_rows, tm):
    n = h2.shape[0] // SUBLANES
    any_spec = pl.BlockSpec(memory_space=pl.ANY)
    return pl.pallas_call(
        functools.partial(_dispatch_kernel, tm=tm, blk=EXPERT_BLK),
        grid_spec=pltpu.PrefetchScalarGridSpec(
            num_scalar_prefetch=4,
            grid=(n // tm,),
            in_specs=[any_spec, pl.BlockSpec((tm * SUBLANES, LANES), lambda i, *_: (i, 0))],
            out_specs=any_spec,
            scratch_shapes=[pltpu.SMEM((TOP_K * tm,), jnp.int32),
                            pltpu.VMEM((EXPERT_BLK * SUBLANES, LANES), h2.dtype),
                            pltpu.SemaphoreType.DMA(()), pltpu.SemaphoreType.DMA(()),
                            pltpu.SemaphoreType.DMA(())]),
        out_shape=jax.ShapeDtypeStruct((n_rows * SUBLANES, LANES), h2.dtype),
        compiler_params=_cparams(1, 24),
    )(counts, padded, starts, n_used, dest_flat, h2)


def _experts_kernel(bexp_ref, nused_ref, xs_ref, win_ref, bin_ref, wout_ref, bout_ref, ys_ref, win_bf, wout_bf):
    i = pl.program_id(0)
    prev = bexp_ref[jnp.maximum(i - 1, 0)]
    new_expert = jnp.logical_or(i == 0, bexp_ref[i] != prev)

    @pl.when(jnp.logical_and(i < nused_ref[0], new_expert))
    def _():
        win_bf[...] = win_ref[0].astype(BF16)
        wout_bf[...] = wout_ref[0].astype(BF16)

    @pl.when(i < nused_ref[0])
    def _():
        dff = wout_bf.shape[0]
        hcat = jnp.dot(_load_rows(xs_ref).astype(BF16), win_bf[...], preferred_element_type=F32) + bin_ref[0]
        x_glu = jnp.minimum(hcat[:, :dff], SWIGLU_LIMIT)
        x_lin = jnp.clip(hcat[:, dff:], -SWIGLU_LIMIT, SWIGLU_LIMIT)
        act = x_glu * (1.0 / (1.0 + jnp.exp(-SWIGLU_ALPHA * x_glu))) * (x_lin + 1.0)
        _store_rows(ys_ref, jnp.dot(act.astype(BF16), wout_bf[...], preferred_element_type=F32) + bout_ref[0])

    @pl.when(i >= nused_ref[0])
    def _():
        ys_ref[...] = jnp.zeros_like(ys_ref)


def _experts_call(block_exp, n_used, xs, w_in, b_in, w_out, b_out):
    n_rows = xs.shape[0] // SUBLANES
    ne, dm, dff2 = w_in.shape
    dff = dff2 // 2
    blk = EXPERT_BLK
    rows = pl.BlockSpec((blk * SUBLANES, LANES), lambda i, be, nu: (jnp.minimum(i, nu[0] - 1), 0))
    return pl.pallas_call(
        _experts_kernel,
        grid_spec=pltpu.PrefetchScalarGridSpec(
            num_scalar_prefetch=2,
            grid=(n_rows // blk,),
            in_specs=[rows,
                      pl.BlockSpec((1, dm, dff2), lambda i, be, nu: (be[i], 0, 0)),
                      pl.BlockSpec((1, 1, dff2), lambda i, be, nu: (be[i], 0, 0)),
                      pl.BlockSpec((1, dff, dm), lambda i, be, nu: (be[i], 0, 0)),
                      pl.BlockSpec((1, 1, dm), lambda i, be, nu: (be[i], 0, 0))],
            out_specs=pl.BlockSpec((blk * SUBLANES, LANES), lambda i, be, nu: (i, 0)),
            scratch_shapes=[pltpu.VMEM((dm, dff2), BF16), pltpu.VMEM((dff, dm), BF16)]),
        out_shape=jax.ShapeDtypeStruct((n_rows * SUBLANES, LANES), F32),
        compiler_params=_cparams(1, 56),
    )(block_exp, n_used, xs, w_in, b_in.reshape(ne, 1, dff2), w_out, b_out.reshape(ne, 1, dm))


def _combine_kernel(dest_hbm, ys_hbm, x1_ref, gate_ref, gt_ref, gf_ref, o_ref, idx_smem, rows, sem_i, sem_d, *, tm):
    i = pl.program_id(0)

    def fetch(step, slot):
        idx_copy = pltpu.make_async_copy(dest_hbm.at[pl.ds(step * (TOP_K * tm), TOP_K * tm)], idx_smem, sem_i)
        idx_copy.start()
        idx_copy.wait()

        def issue(t, c):
            for k in range(TOP_K):
                _row_copy(ys_hbm, idx_smem[k * tm + t], rows.at[slot, k], t, sem_d.at[slot]).start(priority=k % 2)
            return c

        lax.fori_loop(0, tm, issue, 0)

    @pl.when(i == 0)
    def _():
        fetch(0, 0)

    for parity in range(2):
        @pl.when(jnp.logical_and(i + 1 < pl.num_programs(0), (i + 1) % 2 == parity))
        def _():
            fetch(i + 1, parity)

    slot = i % 2

    def drain(t, c):
        for k in range(TOP_K):
            _row_copy(ys_hbm, 0, rows.at[slot, k], 0, sem_d.at[slot]).wait()
        return c

    lax.fori_loop(0, tm, drain, 0)

    gates = gate_ref[...]
    y = gates[:, 0:1] * _load_rows(rows.at[slot, 0])
    for k in range(1, TOP_K):
        y = y + gates[:, k:k + 1] * _load_rows(rows.at[slot, k])
    x2 = x1_ref[...] + gt_ref[0] * y
    ms = jnp.mean(x2 * x2, axis=-1, keepdims=True)
    o_ref[...] = x2 * lax.rsqrt(ms + EPS) * gf_ref[...]


def _combine_call(dest_flat, ys, x1, gate_t, gt2, g_final, tiles_per_batch, tm):
    n, dm = x1.shape
    row = lambda w: pl.BlockSpec((tm, w), lambda i: (i, 0))
    return pl.pallas_call(
        functools.partial(_combine_kernel, tm=tm),
        grid=(n // tm,),
        in_specs=[pl.BlockSpec(memory_space=pl.ANY), pl.BlockSpec(memory_space=pl.ANY),
                  row(dm), row(LANES),
                  pl.BlockSpec((1, 1, dm), lambda i: (i // tiles_per_batch, 0, 0)),
                  pl.BlockSpec((1, dm), lambda i: (0, 0))],
        out_specs=row(dm),
        out_shape=jax.ShapeDtypeStruct((n, dm), F32),
        scratch_shapes=[pltpu.SMEM((TOP_K * tm,), jnp.int32),
                        pltpu.VMEM((2, TOP_K, tm * SUBLANES, LANES), F32),
                        pltpu.SemaphoreType.DMA(()), pltpu.SemaphoreType.DMA((2,))],
        compiler_params=_cparams(1, 40),
    )(dest_flat, ys, x1, gate_t, gt2, g_final)


def _rope_tables(t):
    rows = t // GRID_W
    n_freq = HEAD_DIM // 4
    inv_freq = np.float32(ROPE_THETA) ** (-np.arange(n_freq, dtype=np.float32) / np.float32(n_freq))
    ar = np.arange(rows, dtype=np.float32)[:, None] * inv_freq
    ac = np.arange(GRID_W, dtype=np.float32)[:, None] * inv_freq
    per_row = lambda a: np.broadcast_to(a[:, None, :], (rows, GRID_W, n_freq))
    per_col = lambda a: np.broadcast_to(a[None, :, :], (rows, GRID_W, n_freq))
    cr, sr, cc, sc = per_row(np.cos(ar)), per_row(np.sin(ar)), per_col(np.cos(ac)), per_col(np.sin(ac))
    cos = np.concatenate([cr, cr, cc, cc] * 2, axis=-1).reshape(t, LANES)
    sin = np.concatenate([-sr, sr, -sc, sc] * 2, axis=-1).reshape(t, LANES)
    return jnp.asarray(cos, F32), jnp.asarray(sin, F32)


def kernel(x, c, ctx, c_ctx, w_ada, b_ada, g_attn, w_qkv, gqa_q_norm, gqa_k_norm, diff_lambda,
           diff_subln, w_o, g_ffn, w_router, b_router, w_in, b_in, w_out, b_out, g_final):
    assert w_ada.shape[0] == 1, "single-layer block"
    b, t, dm = x.shape
    n = b * t

    cv = jnp.zeros((8, dm), F32).at[:b].set(c).at[b].set(c_ctx)
    mod = _mod_call(cv, w_ada[0], b_ada[0])
    sh1, sc1, gt1, sh2, sc2, gt2 = [m[:b, None, :] for m in jnp.split(mod, N_MOD, axis=-1)]
    csh1, csc1 = [jnp.broadcast_to(m[b][None, None, :], (b, 1, dm)) for m in jnp.split(mod, N_MOD, axis=-1)[:2]]

    wq = w_qkv[0]
    o_ka, o_va, o_qb, o_kb, o_vb = 512, 640, 768, 1280, 1792
    w_kv = jnp.concatenate([wq[:, o_ka:o_qb], wq[:, o_kb:]], axis=1)
    w_all = jnp.concatenate([w_kv, wq[:, :o_ka], wq[:, o_qb:o_kb]], axis=1).astype(BF16)
    gq = jnp.tile(gqa_q_norm[0], 2)[None, :]
    gk = jnp.tile(gqa_k_norm[0], 2)[None, :]
    lane = jnp.arange(LANES)
    bd = (lane[:, None] // HEAD_DIM == lane[None, :] // HEAD_DIM).astype(BF16)
    cos, sin = _rope_tables(t)
    g_a = g_attn[0][None, :]

    ka, va, kb, vb, qa, qb = _proj_call(x, sh1, sc1, g_a, w_all, cos, sin, gq, gk, bd, rope=True, with_q=True)
    ka_c, va_c, kb_c, vb_c = _proj_call(ctx, csh1, csc1, g_a, w_all[:, :KV_COLS], cos, sin, gq, gk, bd,
                                        rope=False, with_q=False)

    keys = lambda ctx_part, lat_part: jnp.concatenate([ctx_part, lat_part], axis=2)
    att_a = _gqa_call(qa, keys(ka_c, ka), keys(va_c, va))
    att_d = _diff_call(qb, keys(kb_c, kb), keys(vb_c, vb), diff_lambda[0], diff_subln[0][None, :])

    wr_f32 = jnp.zeros((dm, LANES), F32).at[:, :N_EXPERTS].set(w_router[0])
    wr_hi = wr_f32.astype(BF16)
    wr_pad = jnp.stack([wr_hi, (wr_f32 - wr_hi.astype(F32)).astype(BF16)])
    br_pad = jnp.zeros((1, LANES), F32).at[0, :N_EXPERTS].set(b_router[0])
    x1, h2, idx, rank, gate_t, cnt = _oproj_call(
        att_a.reshape(n, -1), att_d.reshape(n, -1), x.reshape(n, dm), gt1, sh2, sc2, g_ffn[0][None, :],
        w_o[0].astype(BF16), wr_pad, br_pad, t // OPROJ_TM)

    blk = EXPERT_BLK
    counts = cnt[:, 0].astype(jnp.int32)
    padded = (counts + blk - 1) // blk * blk
    pad_ends = jnp.cumsum(padded)
    starts = pad_ends - padded
    n_blocks = n * TOP_K // blk + N_EXPERTS
    expert_ids = jnp.arange(N_EXPERTS, dtype=jnp.int32)
    dest = rank + jnp.sum(jnp.where(idx[..., None] == expert_ids, starts, 0), axis=-1)
    tiled = lambda tm: dest.reshape(TOP_K, n // tm, tm).transpose(1, 0, 2).reshape(-1)
    block_row0 = jnp.arange(n_blocks, dtype=jnp.int32) * blk
    block_exp = jnp.minimum(jnp.sum((pad_ends[None, :] <= block_row0[:, None]).astype(jnp.int32), axis=1),
                            N_EXPERTS - 1)
    n_used = (pad_ends[-1:] // blk).astype(jnp.int32)

    dtm = min(DISPATCH_TM, t)
    ctm = min(COMBINE_TM, t)
    xs = _dispatch_call(counts, padded, starts, n_used, tiled(dtm), h2, n_blocks * blk, dtm)
    ys = _experts_call(block_exp, n_used, xs, w_in[0], b_in[0], w_out[0], b_out[0])
    out = _combine_call(tiled(ctm), ys, x1, gate_t, gt2, g_final[None, :], t // ctm, ctm)
    return out.reshape(b, t, dm)
```

```python
import functools

import jax
import jax.numpy as jnp
import numpy as np
from jax import lax
from jax.experimental import pallas as pl
from jax.experimental.pallas import tpu as pltpu

F32 = jnp.float32
BF16 = jnp.bfloat16

HEAD_DIM = 64
GQA_HEADS = 8
GQA_KV_HEADS = 2
GQA_GROUP = GQA_HEADS // GQA_KV_HEADS
DIFF_HEADS = 4
DIFF_V_DIM = 2 * HEAD_DIM
GRID_W = 64
ROPE_THETA = 10000.0
N_EXPERTS = 32
TOP_K = 4
SWIGLU_ALPHA = 1.702
SWIGLU_LIMIT = 7.0
N_MOD = 6
EPS = 1e-6
LAMBDA_INIT = 0.8 - 0.6
LOG2E = 1.4426950408889634
Q_SCALE = HEAD_DIM ** -0.5 * LOG2E

LANES = 128
SUBLANES = 8
GQA_V_EXT = LANES
KV_COLS = 2 * GQA_KV_HEADS * HEAD_DIM + DIFF_HEADS * 2 * HEAD_DIM + DIFF_HEADS * DIFF_V_DIM
Q_COLS = GQA_HEADS * HEAD_DIM + DIFF_HEADS * 2 * HEAD_DIM

PROJ_TM = 512
GQA_TQ = 256
DIFF_TQ = 512
ATT_TK = 2816
MXU_TILE = 256
OPROJ_TM = 512
DISPATCH_TM = 512
COMBINE_TM = 256
EXPERT_BLK = 512


def _store_rows(ref, value):
    m = value.shape[0]
    for s in range(SUBLANES):
        ref[pl.ds(s, m, stride=SUBLANES), :] = value[:, s * LANES:(s + 1) * LANES]


def _load_rows(ref):
    m = ref.shape[0] // SUBLANES
    return jnp.concatenate([ref[pl.ds(s, m, stride=SUBLANES), :] for s in range(SUBLANES)], axis=1)


def _cparams(n_axes, vmem_mb, fuse_inputs=None):
    return pltpu.CompilerParams(dimension_semantics=("arbitrary",) * n_axes,
                                vmem_limit_bytes=vmem_mb * 1024 * 1024, allow_input_fusion=fuse_inputs)


def _mod_kernel(cv_ref, w_ref, b_ref, o_ref):
    cv = cv_ref[...]
    s = cv * (1.0 / (1.0 + jnp.exp(-cv)))
    o_ref[...] = jnp.dot(s, w_ref[...], preferred_element_type=F32,
                         precision=lax.Precision.HIGHEST) + b_ref[...]


def _mod_call(cv, w_ada, b_ada):
    d, n = w_ada.shape
    tn = 1024
    return pl.pallas_call(
        _mod_kernel,
        grid=(n // tn,),
        in_specs=[pl.BlockSpec((8, d), lambda j: (0, 0)),
                  pl.BlockSpec((d, tn), lambda j: (0, j)),
                  pl.BlockSpec((1, tn), lambda j: (0, j))],
        out_specs=pl.BlockSpec((8, tn), lambda j: (0, j)),
        out_shape=jax.ShapeDtypeStruct((8, n), F32),
        compiler_params=_cparams(1, 32),
    )(cv, w_ada, b_ada.reshape(1, n))


def _head_rms(v, gain, bd):
    sq = v * v
    hi = sq.astype(BF16)
    lo = (sq - hi.astype(F32)).astype(BF16)
    ss = (jnp.dot(hi, bd, preferred_element_type=F32) + jnp.dot(lo, bd, preferred_element_type=F32))
    return v * lax.rsqrt(ss * (1.0 / HEAD_DIM) + EPS) * gain


def _rope_chunk(v, cos, sin, first_half):
    partner = jnp.where(first_half, pltpu.roll(v, LANES - 16, 1), pltpu.roll(v, 16, 1))
    return v * cos + partner * sin


def _proj_kernel(x_ref, sh_ref, sc_ref, g_ref, w_ref, cos_ref, sin_ref, gq_ref, gk_ref, bd_ref,
                 *out_refs, rope, with_q):
    xf = x_ref[0]
    ms = jnp.mean(xf * xf, axis=-1, keepdims=True)
    h = xf * lax.rsqrt(ms + EPS) * g_ref[...]
    h = h * (1.0 + sc_ref[0]) + sh_ref[0]
    p = jnp.dot(h.astype(BF16), w_ref[...], preferred_element_type=F32)

    if with_q:
        ka_ref, va_ref, kb_ref, vb_ref, qa_ref, qb_ref = out_refs
    else:
        ka_ref, va_ref, kb_ref, vb_ref = out_refs
    bd = bd_ref[...]
    if rope:
        cos = cos_ref[...]
        sin = sin_ref[...]
        lane = lax.broadcasted_iota(jnp.int32, cos.shape, 1)
        first_half = (lane % 32) < 16

    def chunk(j):
        return p[:, j * LANES:(j + 1) * LANES]

    def maybe_rope(v):
        return _rope_chunk(v, cos, sin, first_half) if rope else v

    def put_heads(ref, j, v):
        ref[0, 2 * j] = v[:, :HEAD_DIM].astype(ref.dtype)
        ref[0, 2 * j + 1] = v[:, HEAD_DIM:].astype(ref.dtype)

    put_heads(ka_ref, 0, maybe_rope(_head_rms(chunk(0), gk_ref[...], bd)))
    lanes = lax.broadcasted_iota(jnp.int32, (p.shape[0], LANES), 1)
    va = chunk(1)
    for h, v in enumerate((va, pltpu.roll(va, HEAD_DIM, 1))):
        va_ref[0, h] = jnp.where(lanes < HEAD_DIM, v, (lanes == HEAD_DIM).astype(F32)).astype(va_ref.dtype)
    for j in range(4):
        put_heads(kb_ref, j, maybe_rope(chunk(2 + j)))
        vb_ref[0, j] = chunk(6 + j).astype(vb_ref.dtype)
    if with_q:
        for j in range(4):
            put_heads(qa_ref, j, maybe_rope(_head_rms(chunk(10 + j), gq_ref[...], bd)) * Q_SCALE)
            put_heads(qb_ref, j, maybe_rope(chunk(14 + j)) * Q_SCALE)


def _proj_call(x, shift, scale, g, w, cos, sin, gq, gk, bd, *, rope, with_q):
    nb, t, d = x.shape
    tm = min(PROJ_TM, t)
    ncols = w.shape[1]
    heads = lambda n, w: pl.BlockSpec((1, n, tm, w), lambda b, i: (b, 0, i, 0))
    head = lambda n: heads(n, HEAD_DIM)
    out_specs = [head(2), heads(2, GQA_V_EXT), head(8), heads(4, DIFF_V_DIM)]
    out_shape = [jax.ShapeDtypeStruct((nb, 2, t, HEAD_DIM), BF16),
                 jax.ShapeDtypeStruct((nb, 2, t, GQA_V_EXT), BF16),
                 jax.ShapeDtypeStruct((nb, 8, t, HEAD_DIM), BF16),
                 jax.ShapeDtypeStruct((nb, 4, t, DIFF_V_DIM), BF16)]
    if with_q:
        out_specs += [head(8), head(8)]
        out_shape += [jax.ShapeDtypeStruct((nb, 8, t, HEAD_DIM), BF16)] * 2
    const = lambda shape: pl.BlockSpec(shape, lambda b, i: (0,) * len(shape))
    return pl.pallas_call(
        functools.partial(_proj_kernel, rope=rope, with_q=with_q),
        grid=(nb, t // tm),
        in_specs=[pl.BlockSpec((1, tm, d), lambda b, i: (b, i, 0)),
                  pl.BlockSpec((1, 1, d), lambda b, i: (b, 0, 0)),
                  pl.BlockSpec((1, 1, d), lambda b, i: (b, 0, 0)),
                  const((1, d)),
                  const((d, ncols)),
                  pl.BlockSpec((tm, LANES), lambda b, i: (i, 0)),
                  pl.BlockSpec((tm, LANES), lambda b, i: (i, 0)),
                  const((1, LANES)), const((1, LANES)), const((LANES, LANES))],
        out_specs=out_specs,
        out_shape=out_shape,
        compiler_params=_cparams(2, 48),
    )(x, shift, scale, g, w, cos, sin, gq, gk, bd)


def _scores(q, k):
    return lax.dot_general(q, k, (((1,), (1,)), ((), ())), preferred_element_type=F32)


def _softmax_step(s, v, carry, sum_on_mxu):
    m, l, acc = carry
    m_new = jnp.maximum(m, jnp.max(s, axis=1, keepdims=True))
    alpha = jnp.exp2(m - m_new)
    p = jnp.exp2(s - m_new)
    if not sum_on_mxu:
        l = alpha * l + jnp.sum(p, axis=1, keepdims=True)
    acc_new = alpha * acc + jnp.dot(p.astype(BF16), v, preferred_element_type=F32)
    return m_new, l, acc_new


def _init_carry(m_rows, dv):
    return (jnp.full((m_rows, 1), -jnp.inf, F32), jnp.zeros((m_rows, 1), F32), jnp.zeros((m_rows, dv), F32))


def _key_tile(n_keys):
    return max(tk for tk in range(MXU_TILE, min(ATT_TK, n_keys) + 1, MXU_TILE) if n_keys % tk == 0)


def _attend(scores, k_count, v_ref, m_rows, dv, tk):
    sum_on_mxu = v_ref.shape[3] > dv
    carry = _init_carry(m_rows, v_ref.shape[3])
    for j in range(k_count // tk):
        rows = pl.ds(j * tk, tk)
        carry = _softmax_step(scores(rows), v_ref[0, 0, rows, :], carry, sum_on_mxu)
    _, l, acc = carry
    return acc[:, :dv] / (acc[:, dv:dv + 1] if sum_on_mxu else l)


def _gqa_kernel(q_ref, k_ref, v_ref, o_ref, *, tk):
    tq = q_ref.shape[2]
    q = q_ref[0].reshape(GQA_GROUP * tq, HEAD_DIM)
    o = _attend(lambda rows: _scores(q, k_ref[0, 0, rows, :]), k_ref.shape[2], v_ref,
                GQA_GROUP * tq, HEAD_DIM, tk).reshape(GQA_GROUP, tq, HEAD_DIM)
    for g in range(GQA_GROUP):
        o_ref[0, :, g * HEAD_DIM:(g + 1) * HEAD_DIM] = o[g].astype(o_ref.dtype)


def _gqa_call(qa, ka, va):
    b, _, t, _ = qa.shape
    nk = ka.shape[2]
    tq = min(GQA_TQ, t)
    kv = lambda w: pl.BlockSpec((1, 1, nk, w), lambda bi, h, i: (bi, h, 0, 0))
    return pl.pallas_call(
        functools.partial(_gqa_kernel, tk=_key_tile(nk)),
        grid=(b, GQA_KV_HEADS, t // tq),
        in_specs=[pl.BlockSpec((1, GQA_GROUP, tq, HEAD_DIM), lambda bi, h, i: (bi, h, i, 0)),
                  kv(HEAD_DIM), kv(va.shape[3])],
        out_specs=pl.BlockSpec((1, tq, GQA_GROUP * HEAD_DIM), lambda bi, h, i: (bi, i, h)),
        out_shape=jax.ShapeDtypeStruct((b, t, GQA_HEADS * HEAD_DIM), BF16),
        compiler_params=_cparams(3, 48, fuse_inputs=[False, True, True]),
    )(qa, ka, va)


def _diff_kernel(q_ref, k_ref, v_ref, lam_ref, sg_ref, o_ref, *, tk):
    tq = q_ref.shape[2]
    q = (q_ref[0, 0], q_ref[0, 1])

    def scores(rows):
        return jnp.concatenate([_scores(q[c], k_ref[0, c, rows, :]) for c in range(2)], axis=0)

    o_both = _attend(scores, k_ref.shape[2], v_ref, 2 * tq, DIFF_V_DIM, tk)
    lamf = lam_ref[...]
    lam = (jnp.exp(jnp.sum(lamf[0:1] * lamf[1:2], axis=-1, keepdims=True))
           - jnp.exp(jnp.sum(lamf[2:3] * lamf[3:4], axis=-1, keepdims=True)) + LAMBDA_INIT)
    o = o_both[:tq] - lam * o_both[tq:]
    ms = jnp.mean(o * o, axis=-1, keepdims=True)
    o = o * lax.rsqrt(ms + EPS) * sg_ref[...] * (1.0 - LAMBDA_INIT)
    o_ref[0] = o.astype(o_ref.dtype)


def _diff_call(qb, kb, vb, lam, sub_g):
    b, _, t, _ = qb.shape
    nk = kb.shape[2]
    tq = min(DIFF_TQ, t)
    return pl.pallas_call(
        functools.partial(_diff_kernel, tk=_key_tile(nk)),
        grid=(b, DIFF_HEADS, t // tq),
        in_specs=[pl.BlockSpec((1, 2, tq, HEAD_DIM), lambda bi, h, i: (bi, h, i, 0)),
                  pl.BlockSpec((1, 2, nk, HEAD_DIM), lambda bi, h, i: (bi, h, 0, 0)),
                  pl.BlockSpec((1, 1, nk, vb.shape[3]), lambda bi, h, i: (bi, h, 0, 0)),
                  pl.BlockSpec((4, HEAD_DIM), lambda bi, h, i: (0, 0)),
                  pl.BlockSpec((1, DIFF_V_DIM), lambda bi, h, i: (0, 0))],
        out_specs=pl.BlockSpec((1, tq, DIFF_V_DIM), lambda bi, h, i: (bi, i, h)),
        out_shape=jax.ShapeDtypeStruct((b, t, DIFF_HEADS * DIFF_V_DIM), BF16),
        compiler_params=_cparams(3, 48, fuse_inputs=[False, True, True, False, False]),
    )(qb, kb, vb, lam, sub_g)


def _oproj_kernel(a_ref, d_ref, x_ref, gt_ref, sh_ref, sc_ref, g_ref, wo_ref, wr_ref, br_ref, tri_ref,
                  x1_ref, h2_ref, idx_ref, rank_ref, gate_ref, cnt_ref):
    i = pl.program_id(0)

    @pl.when(i == 0)
    def _():
        cnt_ref[...] = jnp.zeros_like(cnt_ref)

    half = a_ref.shape[1]
    mix = (jnp.dot(a_ref[...], wo_ref[0:half, :], preferred_element_type=F32)
           + jnp.dot(d_ref[...], wo_ref[half:2 * half, :], preferred_element_type=F32))
    x1 = x_ref[...] + gt_ref[0] * mix
    x1_ref[...] = x1
    ms = jnp.mean(x1 * x1, axis=-1, keepdims=True)
    h2 = x1 * lax.rsqrt(ms + EPS) * g_ref[...]
    h2 = h2 * (1.0 + sc_ref[0]) + sh_ref[0]
    _store_rows(h2_ref, h2)

    h_hi = h2.astype(BF16)
    h_lo = (h2 - h_hi.astype(F32)).astype(BF16)
    logits = (jnp.dot(h_hi, wr_ref[0], preferred_element_type=F32)
              + jnp.dot(h_lo, wr_ref[0], preferred_element_type=F32)
              + jnp.dot(h_hi, wr_ref[1], preferred_element_type=F32)) + br_ref[...]
    lt = logits.T[0:N_EXPERTS, :]
    tm = lt.shape[1]
    eid = lax.broadcasted_iota(jnp.int32, lt.shape, 0).astype(F32)
    vals, sels = [], []
    for k in range(TOP_K):
        m = jnp.max(lt, axis=0, keepdims=True)
        first = jnp.min(jnp.where(lt == m, eid, float(N_EXPERTS)), axis=0, keepdims=True)
        sel = eid == first
        lt = jnp.where(sel, -jnp.inf, lt)
        vals.append(m)
        sels.append(sel)
        idx_ref[k:k + 1, :] = first.astype(jnp.int32)

    onehot = sum(s.astype(F32) for s in sels)
    before = cnt_ref[:, 0:1] + jnp.dot(onehot.astype(BF16), tri_ref[...], preferred_element_type=F32)
    for k in range(TOP_K):
        rank_ref[k:k + 1, :] = jnp.sum(jnp.where(sels[k], before, 0.0), axis=0, keepdims=True).astype(jnp.int32)
    cnt_ref[...] = cnt_ref[...] + jnp.sum(onehot, axis=1, keepdims=True)

    es = [jnp.exp(v - vals[0]) for v in vals]
    den = es[0] + es[1] + es[2] + es[3]
    gates = jnp.concatenate([e / den for e in es] + [jnp.zeros((LANES - TOP_K, tm), F32)], axis=0)
    gate_ref[...] = gates.T


def _oproj_call(a, d, x, gt1, sh2, sc2, g_ffn, w_o, wr_pad, br_pad, tiles_per_batch):
    n, dm = x.shape
    tm = OPROJ_TM
    tri = (lax.broadcasted_iota(jnp.int32, (tm, tm), 0) < lax.broadcasted_iota(jnp.int32, (tm, tm), 1)).astype(BF16)
    row = lambda w: pl.BlockSpec((tm, w), lambda i: (i, 0))
    per_batch = pl.BlockSpec((1, 1, dm), lambda i: (i // tiles_per_batch, 0, 0))
    const = lambda shape: pl.BlockSpec(shape, lambda i: (0,) * len(shape))
    return pl.pallas_call(
        _oproj_kernel,
        grid=(n // tm,),
        in_specs=[row(a.shape[1]), row(d.shape[1]), row(dm), per_batch, per_batch, per_batch,
                  const((1, dm)), const(w_o.shape), const(wr_pad.shape), const((1, LANES)), const((tm, tm))],
        out_specs=[row(dm), pl.BlockSpec((tm * SUBLANES, LANES), lambda i: (i, 0)),
                   pl.BlockSpec((TOP_K, tm), lambda i: (0, i)),
                   pl.BlockSpec((TOP_K, tm), lambda i: (0, i)),
                   row(LANES),
                   const((N_EXPERTS, LANES))],
        out_shape=[jax.ShapeDtypeStruct((n, dm), F32),
                   jax.ShapeDtypeStruct((n * SUBLANES, LANES), F32),
                   jax.ShapeDtypeStruct((TOP_K, n), jnp.int32),
                   jax.ShapeDtypeStruct((TOP_K, n), jnp.int32),
                   jax.ShapeDtypeStruct((n, LANES), F32),
                   jax.ShapeDtypeStruct((N_EXPERTS, LANES), F32)],
        compiler_params=_cparams(1, 48),
    )(a, d, x, gt1, sh2, sc2, g_ffn, w_o, wr_pad, br_pad, tri)


def _row_copy(src_ref, src_row, dst_ref, dst_row, sem):
    tile = lambda r: pl.ds(pl.multiple_of(r * SUBLANES, SUBLANES), SUBLANES)
    return pltpu.make_async_copy(src_ref.at[tile(src_row)], dst_ref.at[tile(dst_row)], sem)


def _dispatch_kernel(cnt_ref, pad_ref, start_ref, nused_ref, dest_hbm, h2_ref, xs_hbm,
                     idx_smem, zeros, sem_i, sem_d, sem_z, *, tm, blk):
    i = pl.program_id(0)

    @pl.when(i == 0)
    def _():
        zeros[...] = jnp.zeros_like(zeros)
        for e in range(N_EXPERTS):
            lo = start_ref[e] + cnt_ref[e]
            hi = start_ref[e] + pad_ref[e]
            lax.fori_loop(lo, hi, lambda r, c: (_row_copy(zeros, 0, xs_hbm, r, sem_z).start(), c)[1], 0)
        for e in range(N_EXPERTS):
            lo = start_ref[e] + cnt_ref[e]
            hi = start_ref[e] + pad_ref[e]
            lax.fori_loop(lo, hi, lambda r, c: (_row_copy(zeros, 0, xs_hbm, r, sem_z).wait(), c)[1], 0)

        def tail_copy(j):
            rows = pl.ds(pl.multiple_of(j * (blk * SUBLANES), blk * SUBLANES), blk * SUBLANES)
            return pltpu.make_async_copy(zeros, xs_hbm.at[rows], sem_z)

        n_blocks = xs_hbm.shape[0] // (blk * SUBLANES)
        lax.fori_loop(nused_ref[0], n_blocks, lambda j, c: (tail_copy(j).start(), c)[1], 0)
        lax.fori_loop(nused_ref[0], n_blocks, lambda j, c: (tail_copy(j).wait(), c)[1], 0)

    idx_copy = pltpu.make_async_copy(dest_hbm.at[pl.ds(i * (TOP_K * tm), TOP_K * tm)], idx_smem, sem_i)
    idx_copy.start()
    idx_copy.wait()

    def issue(t, c):
        for k in range(TOP_K):
            _row_copy(h2_ref, t, xs_hbm, idx_smem[k * tm + t], sem_d).start(priority=k % 2)
        return c

    def drain(t, c):
        for k in range(TOP_K):
            _row_copy(h2_ref, 0, xs_hbm, 0, sem_d).wait()
        return c

    lax.fori_loop(0, tm, issue, 0)
    lax.fori_loop(0, tm, drain, 0)


def _dispatch_call(counts, padded, starts, n_used, dest_flat, h2, n_rows, tm):
    n = h2.shape[0] // SUBLANES
    any_spec = pl.BlockSpec(memory_space=pl.ANY)
    return pl.pallas_call(
        functools.partial(_dispatch_kernel, tm=tm, blk=EXPERT_BLK),
        grid_spec=pltpu.PrefetchScalarGridSpec(
            num_scalar_prefetch=4,
            grid=(n // tm,),
            in_specs=[any_spec, pl.BlockSpec((tm * SUBLANES, LANES), lambda i, *_: (i, 0))],
            out_specs=any_spec,
            scratch_shapes=[pltpu.SMEM((TOP_K * tm,), jnp.int32),
                            pltpu.VMEM((EXPERT_BLK * SUBLANES, LANES), h2.dtype),
                            pltpu.SemaphoreType.DMA(()), pltpu.SemaphoreType.DMA(()),
                            pltpu.SemaphoreType.DMA(())]),
        out_shape=jax.ShapeDtypeStruct((n_rows * SUBLANES, LANES), h2.dtype),
        compiler_params=_cparams(1, 24),
    )(counts, padded, starts, n_used, dest_flat, h2)


def _experts_kernel(bexp_ref, nused_ref, xs_ref, win_ref, bin_ref, wout_ref, bout_ref, ys_ref, win_bf, wout_bf):
    i = pl.program_id(0)
    prev = bexp_ref[jnp.maximum(i - 1, 0)]
    new_expert = jnp.logical_or(i == 0, bexp_ref[i] != prev)

    @pl.when(jnp.logical_and(i < nused_ref[0], new_expert))
    def _():
        win_bf[...] = win_ref[0].astype(BF16)
        wout_bf[...] = wout_ref[0].astype(BF16)

    @pl.when(i < nused_ref[0])
    def _():
        dff = wout_bf.shape[0]
        hcat = jnp.dot(_load_rows(xs_ref).astype(BF16), win_bf[...], preferred_element_type=F32) + bin_ref[0]
        x_glu = jnp.minimum(hcat[:, :dff], SWIGLU_LIMIT)
        x_lin = jnp.clip(hcat[:, dff:], -SWIGLU_LIMIT, SWIGLU_LIMIT)
        act = x_glu * (1.0 / (1.0 + jnp.exp(-SWIGLU_ALPHA * x_glu))) * (x_lin + 1.0)
        _store_rows(ys_ref, jnp.dot(act.astype(BF16), wout_bf[...], preferred_element_type=F32) + bout_ref[0])

    @pl.when(i >= nused_ref[0])
    def _():
        ys_ref[...] = jnp.zeros_like(ys_ref)


def _experts_call(block_exp, n_used, xs, w_in, b_in, w_out, b_out):
    n_rows = xs.shape[0] // SUBLANES
    ne, dm, dff2 = w_in.shape
    dff = dff2 // 2
    blk = EXPERT_BLK
    rows = pl.BlockSpec((blk * SUBLANES, LANES), lambda i, be, nu: (jnp.minimum(i, nu[0] - 1), 0))
    return pl.pallas_call(
        _experts_kernel,
        grid_spec=pltpu.PrefetchScalarGridSpec(
            num_scalar_prefetch=2,
            grid=(n_rows // blk,),
            in_specs=[rows,
                      pl.BlockSpec((1, dm, dff2), lambda i, be, nu: (be[i], 0, 0)),
                      pl.BlockSpec((1, 1, dff2), lambda i, be, nu: (be[i], 0, 0)),
                      pl.BlockSpec((1, dff, dm), lambda i, be, nu: (be[i], 0, 0)),
                      pl.BlockSpec((1, 1, dm), lambda i, be, nu: (be[i], 0, 0))],
            out_specs=pl.BlockSpec((blk * SUBLANES, LANES), lambda i, be, nu: (i, 0)),
            scratch_shapes=[pltpu.VMEM((dm, dff2), BF16), pltpu.VMEM((dff, dm), BF16)]),
        out_shape=jax.ShapeDtypeStruct((n_rows * SUBLANES, LANES), F32),
        compiler_params=_cparams(1, 56),
    )(block_exp, n_used, xs, w_in, b_in.reshape(ne, 1, dff2), w_out, b_out.reshape(ne, 1, dm))


def _combine_kernel(dest_hbm, ys_hbm, x1_ref, gate_ref, gt_ref, gf_ref, o_ref, idx_smem, rows, sem_i, sem_d, *, tm):
    i = pl.program_id(0)

    def fetch(step, slot):
        idx_copy = pltpu.make_async_copy(dest_hbm.at[pl.ds(step * (TOP_K * tm), TOP_K * tm)], idx_smem, sem_i)
        idx_copy.start()
        idx_copy.wait()

        def issue(t, c):
            for k in range(TOP_K):
                _row_copy(ys_hbm, idx_smem[k * tm + t], rows.at[slot, k], t, sem_d.at[slot]).start(priority=k % 2)
            return c

        lax.fori_loop(0, tm, issue, 0)

    @pl.when(i == 0)
    def _():
        fetch(0, 0)

    for parity in range(2):
        @pl.when(jnp.logical_and(i + 1 < pl.num_programs(0), (i + 1) % 2 == parity))
        def _():
            fetch(i + 1, parity)

    slot = i % 2

    def drain(t, c):
        for k in range(TOP_K):
            _row_copy(ys_hbm, 0, rows.at[slot, k], 0, sem_d.at[slot]).wait()
        return c

    lax.fori_loop(0, tm, drain, 0)

    gates = gate_ref[...]
    y = gates[:, 0:1] * _load_rows(rows.at[slot, 0])
    for k in range(1, TOP_K):
        y = y + gates[:, k:k + 1] * _load_rows(rows.at[slot, k])
    x2 = x1_ref[...] + gt_ref[0] * y
    ms = jnp.mean(x2 * x2, axis=-1, keepdims=True)
    o_ref[...] = x2 * lax.rsqrt(ms + EPS) * gf_ref[...]


def _combine_call(dest_flat, ys, x1, gate_t, gt2, g_final, tiles_per_batch, tm):
    n, dm = x1.shape
    row = lambda w: pl.BlockSpec((tm, w), lambda i: (i, 0))
    return pl.pallas_call(
        functools.partial(_combine_kernel, tm=tm),
        grid=(n // tm,),
        in_specs=[pl.BlockSpec(memory_space=pl.ANY), pl.BlockSpec(memory_space=pl.ANY),
                  row(dm), row(LANES),
                  pl.BlockSpec((1, 1, dm), lambda i: (i // tiles_per_batch, 0, 0)),
                  pl.BlockSpec((1, dm), lambda i: (0, 0))],
        out_specs=row(dm),
        out_shape=jax.ShapeDtypeStruct((n, dm), F32),
        scratch_shapes=[pltpu.SMEM((TOP_K * tm,), jnp.int32),
                        pltpu.VMEM((2, TOP_K, tm * SUBLANES, LANES), F32),
                        pltpu.SemaphoreType.DMA(()), pltpu.SemaphoreType.DMA((2,))],
        compiler_params=_cparams(1, 40),
    )(dest_flat, ys, x1, gate_t, gt2, g_final)


def _rope_tables(t):
    rows = t // GRID_W
    n_freq = HEAD_DIM // 4
    inv_freq = np.float32(ROPE_THETA) ** (-np.arange(n_freq, dtype=np.float32) / np.float32(n_freq))
    ar = np.arange(rows, dtype=np.float32)[:, None] * inv_freq
    ac = np.arange(GRID_W, dtype=np.float32)[:, None] * inv_freq
    per_row = lambda a: np.broadcast_to(a[:, None, :], (rows, GRID_W, n_freq))
    per_col = lambda a: np.broadcast_to(a[None, :, :], (rows, GRID_W, n_freq))
    cr, sr, cc, sc = per_row(np.cos(ar)), per_row(np.sin(ar)), per_col(np.cos(ac)), per_col(np.sin(ac))
    cos = np.concatenate([cr, cr, cc, cc] * 2, axis=-1).reshape(t, LANES)
    sin = np.concatenate([-sr, sr, -sc, sc] * 2, axis=-1).reshape(t, LANES)
    return jnp.asarray(cos, F32), jnp.asarray(sin, F32)


def kernel(x, c, ctx, c_ctx, w_ada, b_ada, g_attn, w_qkv, gqa_q_norm, gqa_k_norm, diff_lambda,
           diff_subln, w_o, g_ffn, w_router, b_router, w_in, b_in, w_out, b_out, g_final):
    assert w_ada.shape[0] == 1, "single-layer block"
    b, t, dm = x.shape
    n = b * t

    cv = jnp.zeros((8, dm), F32).at[:b].set(c).at[b].set(c_ctx)
    mod = _mod_call(cv, w_ada[0], b_ada[0])
    sh1, sc1, gt1, sh2, sc2, gt2 = [m[:b, None, :] for m in jnp.split(mod, N_MOD, axis=-1)]
    csh1, csc1 = [jnp.broadcast_to(m[b][None, None, :], (b, 1, dm)) for m in jnp.split(mod, N_MOD, axis=-1)[:2]]

    wq = w_qkv[0]
    o_ka, o_va, o_qb, o_kb, o_vb = 512, 640, 768, 1280, 1792
    w_kv = jnp.concatenate([wq[:, o_ka:o_qb], wq[:, o_kb:]], axis=1)
    w_all = jnp.concatenate([w_kv, wq[:, :o_ka], wq[:, o_qb:o_kb]], axis=1).astype(BF16)
    gq = jnp.tile(gqa_q_norm[0], 2)[None, :]
    gk = jnp.tile(gqa_k_norm[0], 2)[None, :]
    lane = jnp.arange(LANES)
    bd = (lane[:, None] // HEAD_DIM == lane[None, :] // HEAD_DIM).astype(BF16)
    cos, sin = _rope_tables(t)
    g_a = g_attn[0][None, :]

    ka, va, kb, vb, qa, qb = _proj_call(x, sh1, sc1, g_a, w_all, cos, sin, gq, gk, bd, rope=True, with_q=True)
    ka_c, va_c, kb_c, vb_c = _proj_call(ctx, csh1, csc1, g_a, w_all[:, :KV_COLS], cos, sin, gq, gk, bd,
                                        rope=False, with_q=False)

    keys = lambda ctx_part, lat_part: jnp.concatenate([ctx_part, lat_part], axis=2)
    att_a = _gqa_call(qa, keys(ka_c, ka), keys(va_c, va))
    att_d = _diff_call(qb, keys(kb_c, kb), keys(vb_c, vb), diff_lambda[0], diff_subln[0][None, :])

    wr_f32 = jnp.zeros((dm, LANES), F32).at[:, :N_EXPERTS].set(w_router[0])
    wr_hi = wr_f32.astype(BF16)
    wr_pad = jnp.stack([wr_hi, (wr_f32 - wr_hi.astype(F32)).astype(BF16)])
    br_pad = jnp.zeros((1, LANES), F32).at[0, :N_EXPERTS].set(b_router[0])
    x1, h2, idx, rank, gate_t, cnt = _oproj_call(
        att_a.reshape(n, -1), att_d.reshape(n, -1), x.reshape(n, dm), gt1, sh2, sc2, g_ffn[0][None, :],
        w_o[0].astype(BF16), wr_pad, br_pad, t // OPROJ_TM)

    blk = EXPERT_BLK
    counts = cnt[:, 0].astype(jnp.int32)
    padded = (counts + blk - 1) // blk * blk
    pad_ends = jnp.cumsum(padded)
    starts = pad_ends - padded
    n_blocks = n * TOP_K // blk + N_EXPERTS
    expert_ids = jnp.arange(N_EXPERTS, dtype=jnp.int32)
    dest = rank + jnp.sum(jnp.where(idx[..., None] == expert_ids, starts, 0), axis=-1)
    tiled = lambda tm: dest.reshape(TOP_K, n // tm, tm).transpose(1, 0, 2).reshape(-1)
    block_row0 = jnp.arange(n_blocks, dtype=jnp.int32) * blk
    block_exp = jnp.minimum(jnp.sum((pad_ends[None, :] <= block_row0[:, None]).astype(jnp.int32), axis=1),
                            N_EXPERTS - 1)
    n_used = (pad_ends[-1:] // blk).astype(jnp.int32)

    dtm = min(DISPATCH_TM, t)
    ctm = min(COMBINE_TM, t)
    xs = _dispatch_call(counts, padded, starts, n_used, tiled(dtm), h2, n_blocks * blk, dtm)
    ys = _experts_call(block_exp, n_used, xs, w_in[0], b_in[0], w_out[0], b_out[0])
    out = _combine_call(tiled(ctm), ys, x1, gate_t, gt2, g_final[None, :], t // ctm, ctm)
    return out.reshape(b, t, dm)
```
